```python
import jax, jax.numpy as jnp
from jax import lax
import numpy as np

D_MODEL = 1024
BATCH = 16
SEQ = 2048
DEPTH = 4

CTX_LEN = 256
GRID_W = 64
N_MIXERS = 2
EPS = 1e-6
ROPE_THETA = 10000.0
N_A_LAYERS = (DEPTH + 1) // 2
N_B_LAYERS = DEPTH // 2

MLA_HEADS = 16
QK_NOPE = 64
QK_ROPE = 32
V_HEAD = 64
Q_LORA = 256
KV_LORA = 256
QK_HEAD = QK_NOPE + QK_ROPE
MLA_WIDTH = MLA_HEADS * V_HEAD
MLA_IN = Q_LORA + KV_LORA + QK_ROPE + MLA_WIDTH
Q_BLOCK = 128
ATTN_SCALE = 1.0 / math.sqrt(QK_HEAD) if False else QK_HEAD ** -0.5

ML_INNER = 2 * D_MODEL
ML_HEADS = 8
ML_HEAD_DIM = ML_INNER // ML_HEADS
QKV_BLOCK = 4
N_QKV_BLOCKS = ML_INNER // QKV_BLOCK
CONV_K = 5
CHUNK = 64
ML_IN = 3 * ML_INNER

kernel_name = "hybrid_mla_mlstm_prefix_dit"


def rmsnorm(x, g):
    xf = x.astype(jnp.float32)
    r = lax.rsqrt(jnp.mean(xf * xf, axis=-1, keepdims=True) + EPS)
    return (xf * r).astype(x.dtype) * g


def modulation(cvec, w_ada, b_ada):
    m = jax.nn.silu(cvec) @ w_ada + b_ada
    return jnp.split(m, 3, axis=-1)


def axial_rope_tables(n_tokens):
    rows = n_tokens // GRID_W
    row = jnp.repeat(jnp.arange(rows, dtype=jnp.int32), GRID_W).astype(jnp.float32)
    col = jnp.tile(jnp.arange(GRID_W, dtype=jnp.int32), rows).astype(jnp.float32)
    qd = QK_ROPE // 4
    inv = ROPE_THETA ** (-jnp.arange(qd, dtype=jnp.float32) / qd)
    ang = jnp.stack([row[:, None] * inv, col[:, None] * inv], axis=1)
    return jnp.cos(ang), jnp.sin(ang)


def apply_axial_rope(x, cos, sin):
    qd = QK_ROPE // 4
    xr = x.reshape(x.shape[:-1] + (2, 2, qd))
    x1, x2 = xr[..., 0, :], xr[..., 1, :]
    c = cos[:, None].astype(x.dtype)
    s = sin[:, None].astype(x.dtype)
    return jnp.stack([x1 * c - x2 * s, x2 * c + x1 * s], axis=-2).reshape(x.shape)


def mla_project(h, w_in, q_norm, kv_norm, w_uq, w_ukv, cos, sin):
    B, T, _ = h.shape
    u = h @ w_in
    cq, ckv, kr, gate = jnp.split(u, [Q_LORA, Q_LORA + KV_LORA, Q_LORA + KV_LORA + QK_ROPE], axis=-1)
    q = (rmsnorm(cq, q_norm) @ w_uq).reshape(B, T, MLA_HEADS, QK_HEAD)
    kv = (rmsnorm(ckv, kv_norm) @ w_ukv).reshape(B, T, MLA_HEADS, QK_NOPE + V_HEAD)
    q_nope, q_rope = q[..., :QK_NOPE], q[..., QK_NOPE:]
    k_nope, v = kv[..., :QK_NOPE], kv[..., QK_NOPE:]
    kr = kr[:, :, None, :]
    if cos is not None:
        q_rope = apply_axial_rope(q_rope, cos, sin)
        kr = apply_axial_rope(kr, cos, sin)
    q = jnp.concatenate([q_nope, q_rope], axis=-1)
    k = jnp.concatenate([k_nope, jnp.broadcast_to(kr, (B, T, MLA_HEADS, QK_ROPE))], axis=-1)
    return q, k, v, gate


def attend(q, k, v):
    s = jnp.einsum('bqhd,bkhd->bhqk', q, k).astype(jnp.float32) * ATTN_SCALE
    p = jax.nn.softmax(s, axis=-1).astype(v.dtype)
    return jnp.einsum('bhqk,bkhd->bqhd', p, v)


def mla_mixer(hl, hc, w_in, q_norm, kv_norm, w_uq, w_ukv, w_out, cos, sin, need_ctx_out):
    B, T, _ = hl.shape
    ql, kl, vl, gl = mla_project(hl, w_in, q_norm, kv_norm, w_uq, w_ukv, cos, sin)
    qc, kc, vc, gc = mla_project(hc, w_in, q_norm, kv_norm, w_uq, w_ukv, None, None)
    k_all = jnp.concatenate([kc, kl], axis=1)
    v_all = jnp.concatenate([vc, vl], axis=1)
    nb = T // Q_BLOCK
    qb = ql.reshape(B, nb, Q_BLOCK, MLA_HEADS, QK_HEAD).swapaxes(0, 1)
    ob = lax.map(lambda qq: attend(qq, k_all, v_all), qb)
    ol = ob.swapaxes(0, 1).reshape(B, T, MLA_WIDTH)
    yl = (ol * jax.nn.silu(gl)) @ w_out
    yc = None
    if need_ctx_out:
        oc = attend(qc, kc, vc).reshape(B, hc.shape[1], MLA_WIDTH)
        yc = (oc * jax.nn.silu(gc)) @ w_out
    return yl, yc


def centred_dwconv(x, w, b):
    y = lax.conv_general_dilated(x, w[:, None, :], window_strides=(1,),
                                 padding=[(CONV_K // 2, CONV_K // 2)],
                                 dimension_numbers=('NWC', 'WIO', 'NWC'),
                                 feature_group_count=x.shape[-1])
    return y + b


def blockdiag(x, w):
    B, T, _ = x.shape
    xb = x.reshape(B, T, N_QKV_BLOCKS, QKV_BLOCK)
    return jnp.einsum('btnc,ncd->btnd', xb, w).reshape(B, T, ML_INNER)


def mlstm_features(h, w_in, conv_w, conv_b, w_q, w_k, w_v):
    B, T, _ = h.shape
    xm, z, og = jnp.split(h @ w_in, 3, axis=-1)
    xc = jax.nn.silu(centred_dwconv(xm, conv_w, conv_b))
    q = blockdiag(xc, w_q)
    k = blockdiag(xc, w_k)
    v = blockdiag(xm, w_v)
    qkv = jnp.concatenate([q, k, v], axis=-1)
    heads = lambda a: a.reshape(B, T, ML_HEADS, ML_HEAD_DIM)
    return heads(q), heads(k) * (ML_HEAD_DIM ** -0.5), heads(v), qkv, xc, z, jax.nn.sigmoid(og)


def mlstm_gates(qkv, w_if, b_if):
    g = (qkv @ w_if + b_if).astype(jnp.float32)
    li, fpre = g[..., :ML_HEADS], g[..., ML_HEADS:]
    return li, jax.nn.log_sigmoid(fpre)


def mlstm_chunk_scan(q, k, v, li, lf, state, with_output):
    B, T, H, dh = q.shape
    nc = T // CHUNK
    to_c = lambda a: a.reshape(B, nc, CHUNK, H, dh).transpose(1, 0, 3, 2, 4)
    to_g = lambda a: a.reshape(B, nc, CHUNK, H).transpose(1, 0, 3, 2)
    tril = jnp.tril(jnp.ones((CHUNK, CHUNK), dtype=bool))

    def step(carry, xs):
        C, n, m = carry
        qq, kk, vv, ig, fg = xs
        b = jnp.cumsum(fg, axis=-1)
        bL = b[..., -1]
        w_end = bL[..., None] - b + ig
        m_new = jnp.maximum(bL + m, jnp.max(w_end, axis=-1))
        a_end = jnp.exp(w_end - m_new[..., None])
        decay = jnp.exp(bL + m - m_new)
        C_new = decay[..., None, None] * C + jnp.einsum('bhl,bhld,bhle->bhde', a_end, vv, kk)
        n_new = decay[..., None] * n + jnp.einsum('bhl,bhld->bhd', a_end, kk)
        if not with_output:
            return (C_new, n_new, m_new), None
        logD = jnp.where(tril, b[..., :, None] - b[..., None, :] + ig[..., None, :], -jnp.inf)
        m_t = jnp.maximum(b + m[..., None], jnp.max(logD, axis=-1))
        Dm = jnp.exp(logD - m_t[..., None])
        inter = jnp.exp(b + m[..., None] - m_t)
        sqk = jnp.einsum('bhtd,bhsd->bhts', qq, kk) * Dm
        num = jnp.einsum('bhts,bhsd->bhtd', sqk, vv) + inter[..., None] * jnp.einsum('bhde,bhte->bhtd', C, qq)
        den = jnp.sum(sqk, axis=-1) + inter * jnp.einsum('bhd,bhtd->bht', n, qq)
        hh = num / jnp.maximum(jnp.abs(den), jnp.exp(-m_t))[..., None]
        return (C_new, n_new, m_new), hh

    carry, hs = lax.scan(step, state, (to_c(q), to_c(k), to_c(v), to_g(li), to_g(lf)))
    h = None
    if with_output:
        h = hs.transpose(1, 0, 3, 2, 4).reshape(B, T, H, dh)
    return carry, h


def mlstm_direction(q, k, v, li, lf, state, reverse, with_output):
    if reverse:
        q, k, v, li, lf = (jnp.flip(a, axis=1) for a in (q, k, v, li, lf))
    st, h = mlstm_chunk_scan(q, k, v, li, lf, state, with_output)
    if reverse and h is not None:
        h = jnp.flip(h, axis=1)
    return st, h


def mlstm_output(h, o, xc, z, head_norm, skip, w_out):
    B, T = xc.shape[:2]
    hh = o.reshape(B, T, ML_HEADS, ML_HEAD_DIM).astype(jnp.float32) * h
    hn = hh * lax.rsqrt(jnp.mean(hh * hh, axis=-1, keepdims=True) + EPS)
    hn = hn.reshape(B, T, ML_INNER).astype(xc.dtype) * head_norm
    return ((hn + skip * xc) * jax.nn.silu(z)) @ w_out


def mlstm_mixer(hl, hc, w_in, conv_w, conv_b, w_q, w_k, w_v, w_if, b_if, head_norm, skip, w_out, need_ctx_out):
    B = hl.shape[0]
    ql, kl, vl, qkvl, xcl, zl, ol = mlstm_features(hl, w_in, conv_w, conv_b, w_q, w_k, w_v)
    qc, kc, vc, qkvc, xcc, zc, oc = mlstm_features(hc, w_in, conv_w, conv_b, w_q, w_k, w_v)
    zero_state = (jnp.zeros((B, ML_HEADS, ML_HEAD_DIM, ML_HEAD_DIM), jnp.float32),
                  jnp.zeros((B, ML_HEADS, ML_HEAD_DIM), jnp.float32),
                  jnp.zeros((B, ML_HEADS), jnp.float32))
    h_lat = jnp.zeros(ql.shape, jnp.float32)
    h_ctx = jnp.zeros(qc.shape, jnp.float32)
    for d, reverse in ((0, False), (1, True)):
        li_c, lf_c = mlstm_gates(qkvc, w_if[d], b_if[d])
        st_c, hc_d = mlstm_direction(qc, kc, vc, li_c, lf_c, zero_state, reverse, need_ctx_out)
        li_l, lf_l = mlstm_gates(qkvl, w_if[d], b_if[d])
        _, hl_d = mlstm_direction(ql, kl, vl, li_l, lf_l, st_c, reverse, True)
        h_lat = h_lat + hl_d
        if need_ctx_out:
            h_ctx = h_ctx + hc_d
    yl = mlstm_output(h_lat, ol, xcl, zl, head_norm, skip, w_out)
    yc = mlstm_output(h_ctx, oc, xcc, zc, head_norm, skip, w_out) if need_ctx_out else None
    return yl, yc


def setup_inputs(seed: int = 0) -> dict:
    key = jax.random.key(seed)
    ks = iter(jax.random.split(key, 40))
    nrm = lambda shape, scale: jax.random.normal(next(ks), shape, jnp.float32) * scale
    A, Bn = N_A_LAYERS, N_B_LAYERS
    b_if = jnp.concatenate([nrm((Bn, 2, ML_HEADS), 0.1),
                            jnp.linspace(3.0, 6.0, ML_HEADS, dtype=jnp.float32) + nrm((Bn, 2, ML_HEADS), 0.01)], axis=-1)
    return {
        "x": nrm((BATCH, SEQ, D_MODEL), 1.0),
        "c": nrm((BATCH, D_MODEL), 1.0),
        "ctx": nrm((BATCH, CTX_LEN, D_MODEL), 1.0),
        "c_ctx": nrm((D_MODEL,), 1.0),
        "ada_w": nrm((DEPTH, D_MODEL, 3 * D_MODEL), 0.5 * D_MODEL ** -0.5),
        "ada_b": nrm((DEPTH, 3 * D_MODEL), 0.02),
        "norm_pre": 1.0 + nrm((DEPTH, D_MODEL), 0.01),
        "norm_post": 1.0 + nrm((DEPTH, D_MODEL), 0.01),
        "mla_w_in": nrm((A, D_MODEL, MLA_IN), D_MODEL ** -0.5),
        "mla_q_norm": 1.0 + nrm((A, Q_LORA), 0.01),
        "mla_kv_norm": 1.0 + nrm((A, KV_LORA), 0.01),
        "mla_w_uq": nrm((A, Q_LORA, MLA_HEADS * QK_HEAD), Q_LORA ** -0.5),
        "mla_w_ukv": nrm((A, KV_LORA, MLA_HEADS * (QK_NOPE + V_HEAD)), KV_LORA ** -0.5),
        "mla_w_out": nrm((A, MLA_WIDTH, D_MODEL), MLA_WIDTH ** -0.5),
        "ml_w_in": nrm((Bn, D_MODEL, ML_IN), D_MODEL ** -0.5),
        "ml_conv_w": nrm((Bn, CONV_K, ML_INNER), CONV_K ** -0.5),
        "ml_conv_b": nrm((Bn, ML_INNER), 0.02),
        "ml_w_q": nrm((Bn, N_QKV_BLOCKS, QKV_BLOCK, QKV_BLOCK), QKV_BLOCK ** -0.5),
        "ml_w_k": nrm((Bn, N_QKV_BLOCKS, QKV_BLOCK, QKV_BLOCK), QKV_BLOCK ** -0.5),
        "ml_w_v": nrm((Bn, N_QKV_BLOCKS, QKV_BLOCK, QKV_BLOCK), QKV_BLOCK ** -0.5),
        "ml_w_if": nrm((Bn, 2, 3 * ML_INNER, 2 * ML_HEADS), 0.5 * (3 * ML_INNER) ** -0.5),
        "ml_b_if": b_if,
        "ml_head_norm": 1.0 + nrm((Bn, ML_INNER), 0.01),
        "ml_skip": 1.0 + nrm((Bn, ML_INNER), 0.01),
        "ml_w_out": nrm((Bn, ML_INNER, D_MODEL), ML_INNER ** -0.5),
    }


def reference(x, c, ctx, c_ctx, ada_w, ada_b, norm_pre, norm_post,
              mla_w_in, mla_q_norm, mla_kv_norm, mla_w_uq, mla_w_ukv, mla_w_out,
              ml_w_in, ml_conv_w, ml_conv_b, ml_w_q, ml_w_k, ml_w_v, ml_w_if, ml_b_if,
              ml_head_norm, ml_skip, ml_w_out):
    T = x.shape[1]
    cos, sin = axial_rope_tables(T)
    for i in range(DEPTH):
        need_ctx_out = i < DEPTH - 1
        sh_l, sc_l, g_l = modulation(c, ada_w[i], ada_b[i])
        sh_c, sc_c, g_c = modulation(c_ctx, ada_w[i], ada_b[i])
        hl = rmsnorm(x, norm_pre[i]) * (1.0 + sc_l[:, None]) + sh_l[:, None]
        hc = rmsnorm(ctx, norm_pre[i]) * (1.0 + sc_c) + sh_c
        j = i // N_MIXERS
        if i % N_MIXERS == 0:
            yl, yc = mla_mixer(hl, hc, mla_w_in[j], mla_q_norm[j], mla_kv_norm[j], mla_w_uq[j],
                               mla_w_ukv[j], mla_w_out[j], cos, sin, need_ctx_out)
        else:
            yl, yc = mlstm_mixer(hl, hc, ml_w_in[j], ml_conv_w[j], ml_conv_b[j], ml_w_q[j], ml_w_k[j],
                                 ml_w_v[j], ml_w_if[j], ml_b_if[j], ml_head_norm[j], ml_skip[j],
                                 ml_w_out[j], need_ctx_out)
        x = x + g_l[:, None] * rmsnorm(yl, norm_post[i])
        if need_ctx_out:
            ctx = ctx + g_c * rmsnorm(yc, norm_post[i])
    return x
```

```python
import functools

import jax
import jax.numpy as jnp
from jax import lax
from jax.experimental import pallas as pl
from jax.experimental.pallas import tpu as pltpu

F32 = jnp.float32
BF16 = jnp.bfloat16

D_MODEL = 1024
DEPTH = 4
GRID_W = 64
EPS = 1e-6
ROPE_THETA = 10000.0
MLA_HEADS = 16
QK_NOPE = 64
QK_ROPE = 32
V_HEAD = 64
Q_LORA = 256
KV_LORA = 256
QK_HEAD = QK_NOPE + QK_ROPE
MLA_WIDTH = MLA_HEADS * V_HEAD
ATTN_SCALE = QK_HEAD ** -0.5
ML_INNER = 2 * D_MODEL
ML_HEADS = 8
ML_HEAD_DIM = ML_INNER // ML_HEADS
QKV_BLOCK = 4
CONV_K = 5

LANES = 128
MXU_DIM = 256
VMEM_LIMIT = 56 * 1024 * 1024

HEAD_PAD = LANES
MLA_QW = MLA_HEADS * HEAD_PAD
SCAN_CHUNK = MXU_DIM
HALO = 16
GATE_LANES = 8

NT_DIMS = (((1,), (1,)), ((), ()))
TN_DIMS = (((0,), (0,)), ((), ()))


def _cparams(n_axes):
    return pltpu.CompilerParams(dimension_semantics=("parallel",) * n_axes,
                                vmem_limit_bytes=VMEM_LIMIT)


def _const_spec(shape):
    nd = len(shape)
    return pl.BlockSpec(shape, lambda *_: (0,) * nd)


def _rms(x, g):
    r = lax.rsqrt(jnp.mean(x * x, axis=-1, keepdims=True) + EPS)
    return (x * r) * g


def _silu(x):
    return x * jax.nn.sigmoid(x)


def _split3(x):
    x1 = x.astype(BF16)
    r1 = x - x1.astype(F32)
    x2 = r1.astype(BF16)
    x3 = (r1 - x2.astype(F32)).astype(BF16)
    return x1, x2, x3


def _dot(a, b):
    return jnp.dot(a, b, preferred_element_type=F32)


def _mod_kernel(c_ref, w_ref, b_ref, o_ref):
    a1, a2, a3 = _split3(_silu(c_ref[...]))
    w = w_ref[0]
    w1 = w.astype(BF16)
    w2 = (w - w1.astype(F32)).astype(BF16)
    acc = _dot(a1, w1) + (_dot(a1, w2) + _dot(a2, w1)) + (_dot(a2, w2) + _dot(a3, w1))
    o_ref[0] = acc + b_ref[0]


def _modulation(cc, ada_w, ada_b):
    rows = cc.shape[0]
    nblk = ada_w.shape[2] // D_MODEL
    return pl.pallas_call(
        _mod_kernel,
        grid=(DEPTH, nblk),
        in_specs=[pl.BlockSpec((rows, D_MODEL), lambda i, j: (0, 0)),
                  pl.BlockSpec((1, D_MODEL, D_MODEL), lambda i, j: (i, 0, j)),
                  pl.BlockSpec((1, 1, D_MODEL), lambda i, j: (i, 0, j))],
        out_specs=pl.BlockSpec((1, rows, D_MODEL), lambda i, j: (i, 0, j)),
        out_shape=jax.ShapeDtypeStruct((DEPTH, rows, ada_w.shape[2]), F32),
        compiler_params=_cparams(2),
        name="modulation",
    )(cc, ada_w, ada_b.reshape(DEPTH, 1, -1))


def _prenorm(x_ref, mod_ref, npre_ref):
    h = _rms(x_ref[0], npre_ref[...]) * (1.0 + mod_ref[0, 1:2, :]) + mod_ref[0, 0:1, :]
    return h.astype(BF16)


def _mla_proj_kernel(x_ref, mod_ref, npre_ref, win_ref, qn_ref, kvn_ref, wuq_ref, wukv_ref,
                     tabq_ref, tabk_ref, q_ref, k_ref, v_ref, sg_ref):
    h = _prenorm(x_ref, mod_ref, npre_ref)
    u = _dot(h, win_ref[...])
    cq = u[:, :Q_LORA]
    ckv = u[:, Q_LORA:Q_LORA + KV_LORA]
    krp = u[:, Q_LORA + KV_LORA:Q_LORA + KV_LORA + LANES]
    gate = u[:, Q_LORA + KV_LORA + LANES:]
    q = _dot(_rms(cq, qn_ref[...]).astype(BF16), wuq_ref[...])
    kv = _dot(_rms(ckv, kvn_ref[...]).astype(BF16), wukv_ref[...])
    t = krp * tabk_ref[...]
    r = t + pltpu.roll(t, QK_ROPE, 1)
    lane = lax.broadcasted_iota(jnp.int32, r.shape, 1)
    rk = jnp.where(lane >= QK_NOPE, r, 0.0)
    tabq = tabq_ref[...]
    for hh in range(MLA_HEADS):
        sl = slice(HEAD_PAD * hh, HEAD_PAD * (hh + 1))
        q_ref[0, :, sl] = (q[:, sl] * tabq).astype(BF16)
        k_ref[0, :, sl] = (kv[:, sl] + rk).astype(BF16)
    v_ref[0] = kv[:, MLA_QW:].astype(BF16)
    sg_ref[0] = _silu(gate).astype(BF16)


def _mla_project(x, mod, npre, w, tabq, tabk, tm):
    B, T, D = x.shape
    nin = w["w_in"].shape[1]
    nkv = w["w_ukv"].shape[1]
    row = lambda width: pl.BlockSpec((1, tm, width), lambda b, i: (b, i, 0))
    return pl.pallas_call(
        _mla_proj_kernel,
        grid=(B, T // tm),
        in_specs=[row(D),
                  pl.BlockSpec((1, 2, D), lambda b, i: (b, 0, 0)),
                  _const_spec((1, D)),
                  _const_spec((D, nin)),
                  _const_spec((1, Q_LORA)),
                  _const_spec((1, KV_LORA)),
                  _const_spec((Q_LORA, MLA_QW)),
                  _const_spec((KV_LORA, nkv)),
                  pl.BlockSpec((tm, LANES), lambda b, i: (i, 0)),
                  pl.BlockSpec((tm, LANES), lambda b, i: (i, 0))],
        out_specs=[row(MLA_QW), row(MLA_QW), row(MLA_WIDTH), row(MLA_WIDTH)],
        out_shape=[jax.ShapeDtypeStruct((B, T, MLA_QW), BF16),
                   jax.ShapeDtypeStruct((B, T, MLA_QW), BF16),
                   jax.ShapeDtypeStruct((B, T, MLA_WIDTH), BF16),
                   jax.ShapeDtypeStruct((B, T, MLA_WIDTH), BF16)],
        compiler_params=_cparams(2),
        name="mla_project",
    )(x, mod, npre, w["w_in"], w["q_norm"], w["kv_norm"], w["w_uq"], w["w_ukv"], tabq, tabk)


def _attn_kernel(*refs, n_seg, tq):
    q_ref = refs[0]
    k_refs = refs[1:1 + n_seg]
    v_refs = refs[1 + n_seg:1 + 2 * n_seg]
    sg_ref = refs[1 + 2 * n_seg]
    o_ref = refs[2 + 2 * n_seg]
    n_q = q_ref.shape[1] // tq
    lane = lax.broadcasted_iota(jnp.int32, (tq, LANES), 1)

    def body(i, carry):
        rows = pl.ds(pl.multiple_of(i * tq, tq), tq)
        outs = []
        for hh in range(2):
            sl = slice(HEAD_PAD * hh, HEAD_PAD * (hh + 1))
            qb = q_ref[0, rows, sl]
            ss = [lax.dot_general(qb, k[0, :, sl], NT_DIMS, preferred_element_type=F32) for k in k_refs]
            m = functools.reduce(jnp.maximum, [jnp.max(s, axis=-1, keepdims=True) for s in ss])
            ps = [jnp.exp(s - m) for s in ss]
            l = functools.reduce(jnp.add, [jnp.sum(p, axis=-1, keepdims=True) for p in ps])
            o = functools.reduce(jnp.add, [_dot(p.astype(BF16), v[0]) for p, v in zip(ps, v_refs)])
            outs.append(o / l)
        o = jnp.where(lane < V_HEAD, outs[0], outs[1])
        o_ref[0, rows, :] = (o * sg_ref[0, rows, :].astype(F32)).astype(BF16)
        return carry

    lax.fori_loop(0, n_q, body, 0)


def _attention(q, ks, vs, sg, tq):
    B, Tq, _ = q.shape
    n_seg = len(ks)
    pair = lambda t, width: pl.BlockSpec((1, t, width), lambda b, p: (b, 0, p))
    return pl.pallas_call(
        functools.partial(_attn_kernel, n_seg=n_seg, tq=tq),
        grid=(B, MLA_HEADS // 2),
        in_specs=([pair(Tq, 2 * HEAD_PAD)]
                  + [pair(k.shape[1], 2 * HEAD_PAD) for k in ks]
                  + [pair(v.shape[1], 2 * V_HEAD) for v in vs]
                  + [pair(Tq, 2 * V_HEAD)]),
        out_specs=pair(Tq, 2 * V_HEAD),
        out_shape=jax.ShapeDtypeStruct((B, Tq, MLA_WIDTH), BF16),
        compiler_params=_cparams(2),
        name="mla_attention",
    )(q, *ks, *vs, sg)


def _out_kernel(a_ref, w_ref, x_ref, g_ref, npost_ref, o_ref):
    y = _dot(a_ref[0], w_ref[...])
    o_ref[0] = x_ref[0] + g_ref[0] * _rms(y, npost_ref[...])


def _out_project(a, w_out, x, gate, npost, tm):
    B, T, K = a.shape
    D = x.shape[-1]
    return pl.pallas_call(
        _out_kernel,
        grid=(B, T // tm),
        in_specs=[pl.BlockSpec((1, tm, K), lambda b, i: (b, i, 0)),
                  _const_spec((K, D)),
                  pl.BlockSpec((1, tm, D), lambda b, i: (b, i, 0)),
                  pl.BlockSpec((1, 1, D), lambda b, i: (b, 0, 0)),
                  _const_spec((1, D))],
        out_specs=pl.BlockSpec((1, tm, D), lambda b, i: (b, i, 0)),
        out_shape=jax.ShapeDtypeStruct(x.shape, F32),
        compiler_params=_cparams(2),
        name="out_project",
    )(a, w_out, x, gate, npost)


def _ml_inproj_kernel(x_ref, mod_ref, npre_ref, w_ref, xm_ref, sz_ref, og_ref):
    h = _prenorm(x_ref, mod_ref, npre_ref)
    xm_ref[0] = _dot(h, w_ref[:, :ML_INNER]).astype(BF16)
    sz_ref[0] = _silu(_dot(h, w_ref[:, ML_INNER:2 * ML_INNER])).astype(BF16)
    og_ref[0] = jax.nn.sigmoid(_dot(h, w_ref[:, 2 * ML_INNER:])).astype(BF16)


def _ml_inproject(x, mod, npre, w_in, tm):
    B, T, D = x.shape
    row = lambda width: pl.BlockSpec((1, tm, width), lambda b, i: (b, i, 0))
    return pl.pallas_call(
        _ml_inproj_kernel,
        grid=(B, T // tm),
        in_specs=[row(D),
                  pl.BlockSpec((1, 2, D), lambda b, i: (b, 0, 0)),
                  _const_spec((1, D)),
                  _const_spec((D, 3 * ML_INNER))],
        out_specs=[row(ML_INNER)] * 3,
        out_shape=[jax.ShapeDtypeStruct((B, T, ML_INNER), BF16)] * 3,
        compiler_params=_cparams(2),
        name="mlstm_inproject",
    )(x, mod, npre, w_in)


def _ml_feat_kernel(xm_ref, xp_ref, xn_ref, cw_ref, cb_ref, wq_ref, wk_ref, wv_ref, wif_ref, bif_ref,
                    trif_ref, trib_ref, q_ref, k_ref, v_ref, xc_ref, gc_ref, gr_ref, xe_ref, *, tm):
    i = pl.program_id(1)
    last = pl.num_programs(1) - 1
    pad = CONV_K // 2
    xm = xm_ref[0]
    half = HALO // 2
    xe_ref[0:half, :] = jnp.where(i > 0, xp_ref[0, half:, :].astype(F32), 0.0)
    xe_ref[half:half + tm, :] = xm.astype(F32)
    xe_ref[half + tm:, :] = jnp.where(i < last, xn_ref[0, :half, :].astype(F32), 0.0)
    acc = cb_ref[...] + cw_ref[0:1, :] * xe_ref[half - pad:half - pad + tm, :]
    for j in range(1, CONV_K):
        acc = acc + cw_ref[j:j + 1, :] * xe_ref[half - pad + j:half - pad + j + tm, :]
    xc = _silu(acc)
    xcb = xc.astype(BF16)
    xc_ref[0] = xcb

    n_tiles = ML_INNER // MXU_DIM
    g = jnp.zeros((tm, LANES), F32) + bif_ref[...]
    for j in range(n_tiles):
        sl = slice(MXU_DIM * j, MXU_DIM * (j + 1))
        qj = _dot(xcb[:, sl], wq_ref[j]).astype(BF16)
        kj = _dot(xcb[:, sl], wk_ref[j])
        vj = _dot(xm[:, sl], wv_ref[j]).astype(BF16)
        q_ref[0, :, sl] = qj
        k_ref[0, :, sl] = (kj * (ML_HEAD_DIM ** -0.5)).astype(BF16)
        v_ref[0, :, sl] = vj
        g = g + _dot(qj, wif_ref[sl, :])
        g = g + _dot(kj.astype(BF16), wif_ref[ML_INNER + MXU_DIM * j:ML_INNER + MXU_DIM * (j + 1), :])
        g = g + _dot(vj, wif_ref[2 * ML_INNER + MXU_DIM * j:2 * ML_INNER + MXU_DIM * (j + 1), :])

    lane = lax.broadcasted_iota(jnp.int32, g.shape, 1)
    sub = lane % GATE_LANES
    is_f_fwd = sub == 1
    is_f_bwd = sub == 3
    logsig = jnp.minimum(g, 0.0) - jnp.log(1.0 + jnp.exp(-jnp.abs(g)))
    gs = jnp.where(is_f_fwd | is_f_bwd, logsig, g)
    g1, g2, g3 = _split3(gs)
    cum_f = _dot(trif_ref[...], g1) + _dot(trif_ref[...], g2) + _dot(trif_ref[...], g3)
    cum_b = _dot(trib_ref[...], g1) + _dot(trib_ref[...], g2) + _dot(trib_ref[...], g3)
    out = jnp.where(is_f_fwd, cum_f, jnp.where(is_f_bwd, cum_b, gs))
    for h in range(ML_HEADS):
        gc_ref[0, h] = out[:, GATE_LANES * h:GATE_LANES * (h + 1)]
    gr_ref[0] = out.T[:ML_HEADS * GATE_LANES, :]


def _ml_features(xm, w, tm):
    B, T, C = xm.shape
    nh = tm // HALO
    n_halo = T // HALO
    row = pl.BlockSpec((1, tm, C), lambda b, i: (b, i, 0))
    n_tiles = C // MXU_DIM
    return pl.pallas_call(
        functools.partial(_ml_feat_kernel, tm=tm),
        grid=(B, T // tm),
        in_specs=[row,
                  pl.BlockSpec((1, HALO, C), lambda b, i: (b, jnp.maximum(i * nh - 1, 0), 0)),
                  pl.BlockSpec((1, HALO, C), lambda b, i: (b, jnp.minimum((i + 1) * nh, n_halo - 1), 0)),
                  _const_spec((CONV_K, C)),
                  _const_spec((1, C)),
                  _const_spec((n_tiles, MXU_DIM, MXU_DIM)),
                  _const_spec((n_tiles, MXU_DIM, MXU_DIM)),
                  _const_spec((n_tiles, MXU_DIM, MXU_DIM)),
                  _const_spec((3 * C, LANES)),
                  _const_spec((1, LANES)),
                  _const_spec((tm, tm)),
                  _const_spec((tm, tm))],
        out_specs=[row, row, row, row,
                   pl.BlockSpec((1, ML_HEADS, tm, GATE_LANES), lambda b, i: (b, 0, i, 0)),
                   pl.BlockSpec((1, ML_HEADS * GATE_LANES, tm), lambda b, i: (b, 0, i))],
        out_shape=[jax.ShapeDtypeStruct((B, T, C), BF16)] * 4
                  + [jax.ShapeDtypeStruct((B, ML_HEADS, T, GATE_LANES), F32),
                     jax.ShapeDtypeStruct((B, ML_HEADS * GATE_LANES, T), F32)],
        scratch_shapes=[pltpu.VMEM((tm + HALO, C), F32)],
        compiler_params=_cparams(2),
        name="mlstm_features",
    )(xm, xm, xm, w["conv_w"], w["conv_b"], w["w_q"], w["w_k"], w["w_v"], w["w_if"], w["b_if"],
      w["tri_f"], w["tri_b"])


def _scan_chunk(q_ref, k_ref, v_ref, gc_ref, gr_ref, h_ref, ct_ref, n, m, r0, d, with_out, accumulate):
    L = SCAN_CHUNK
    rows = pl.ds(r0, L)
    kk = k_ref[0, rows, :]
    vv = v_ref[0, rows, :]
    gcol = gc_ref[0, 0, rows, :]
    ig_c = gcol[:, 2 * d:2 * d + 1]
    b_c = gcol[:, 2 * d + 1:2 * d + 2]
    ig_r = gr_ref[0, 2 * d:2 * d + 1, rows]
    b_r = gr_ref[0, 2 * d + 1:2 * d + 2, rows]
    b_end = b_r[:, L - 1:L] if d == 0 else b_r[:, 0:1]
    m_new = jnp.maximum(b_end + m, jnp.max(b_end - b_r + ig_r, axis=-1, keepdims=True))
    a_c = jnp.exp(b_end - b_c + ig_c - m_new)
    decay = jnp.exp(b_end + m - m_new)
    ct = ct_ref[...]
    if with_out:
        qq = q_ref[0, rows, :]
        ti = lax.broadcasted_iota(jnp.int32, (L, L), 0)
        si = lax.broadcasted_iota(jnp.int32, (L, L), 1)
        mask = (si <= ti) if d == 0 else (si >= ti)
        log_d = jnp.where(mask, b_c - b_r + ig_r, -jnp.inf)
        m_t = jnp.maximum(b_c + m, jnp.max(log_d, axis=-1, keepdims=True))
        inter = jnp.exp(b_c + m - m_t)
        s = lax.dot_general(qq, kk, NT_DIMS, preferred_element_type=F32) * jnp.exp(log_d - m_t)
        num = _dot(s.astype(BF16), vv) + inter * _dot(qq, ct.astype(BF16))
        den = (jnp.sum(s, axis=-1, keepdims=True)
               + inter * jnp.sum(qq.astype(F32) * n, axis=-1, keepdims=True))
        hh = num / jnp.maximum(jnp.abs(den), jnp.exp(-m_t))
        if accumulate:
            h_ref[rows, :] += hh
        else:
            h_ref[rows, :] = hh
    av = (a_c * vv.astype(F32)).astype(BF16)
    ct_ref[...] = decay * ct + lax.dot_general(kk, av, TN_DIMS, preferred_element_type=F32)
    n_new = decay * n + jnp.sum(a_c * kk.astype(F32), axis=0, keepdims=True)
    return n_new, m_new


def _scan_finish(h_ref, og_ref, xc_ref, sz_ref, hn_ref, skip_ref, y_ref):
    L = SCAN_CHUNK

    def body(i, carry):
        rows = pl.ds(pl.multiple_of(i * L, L), L)
        hh = og_ref[0, rows, :].astype(F32) * h_ref[rows, :]
        hn = hh * lax.rsqrt(jnp.mean(hh * hh, axis=-1, keepdims=True) + EPS)
        y = (hn * hn_ref[...] + skip_ref[...] * xc_ref[0, rows, :].astype(F32)) * sz_ref[0, rows, :].astype(F32)
        y_ref[0, rows, :] = y.astype(BF16)
        return carry

    lax.fori_loop(0, h_ref.shape[0] // L, body, 0)


def _ml_scan_kernel(*refs, ctx_out):
    (ql, kl, vl, gcl, grl, ogl, xcl, szl, qc, kc, vc, gcc, grc) = refs[:13]
    pos = 13
    if ctx_out:
        ogc, xcc, szc = refs[pos:pos + 3]
        pos += 3
    hn_ref, skip_ref = refs[pos:pos + 2]
    pos += 2
    yl_ref = refs[pos]
    pos += 1
    if ctx_out:
        yc_ref = refs[pos]
        pos += 1
    hl_ref, hc_ref, ct_ref = refs[pos:pos + 3]
    L = SCAN_CHUNK
    n_l = ql.shape[1] // L
    n_c = kc.shape[1] // L

    for d in range(2):
        ct_ref[...] = jnp.zeros_like(ct_ref)
        n = jnp.zeros((1, ML_HEAD_DIM), F32)
        m = jnp.zeros((1, 1), F32)
        for c in (range(n_c) if d == 0 else reversed(range(n_c))):
            n, m = _scan_chunk(qc, kc, vc, gcc, grc, hc_ref, ct_ref, n, m, c * L, d, ctx_out, d == 1)

        def body(i, carry, d=d):
            c = i if d == 0 else n_l - 1 - i
            r0 = pl.multiple_of(c * L, L)
            return _scan_chunk(ql, kl, vl, gcl, grl, hl_ref, ct_ref, carry[0], carry[1], r0, d, True, d == 1)

        lax.fori_loop(0, n_l, body, (n, m))

    _scan_finish(hl_ref, ogl, xcl, szl, hn_ref, skip_ref, yl_ref)
    if ctx_out:
        _scan_finish(hc_ref, ogc, xcc, szc, hn_ref, skip_ref, yc_ref)


def _ml_scan(lat, ctx, head_norm, skip, ctx_out):
    B, T, C = lat["q"].shape
    Tc = ctx["q"].shape[1]
    dh = ML_HEAD_DIM

    def stream_specs(t, names):
        specs = []
        for nm in names:
            if nm == "gc":
                specs.append(pl.BlockSpec((1, 1, t, GATE_LANES), lambda b, h: (b, h, 0, 0)))
            elif nm == "gr":
                specs.append(pl.BlockSpec((1, GATE_LANES, t), lambda b, h: (b, h, 0)))
            else:
                specs.append(pl.BlockSpec((1, t, dh), lambda b, h: (b, 0, h)))
        return specs

    lat_names = ["q", "k", "v", "gc", "gr", "og", "xc", "sz"]
    ctx_names = ["q", "k", "v", "gc", "gr"] + (["og", "xc", "sz"] if ctx_out else [])
    head_vec = pl.BlockSpec((1, dh), lambda b, h: (0, h))
    out_specs = [pl.BlockSpec((1, T, dh), lambda b, h: (b, 0, h))]
    out_shape = [jax.ShapeDtypeStruct((B, T, C), BF16)]
    if ctx_out:
        out_specs.append(pl.BlockSpec((1, Tc, dh), lambda b, h: (b, 0, h)))
        out_shape.append(jax.ShapeDtypeStruct((B, Tc, C), BF16))
    outs = pl.pallas_call(
        functools.partial(_ml_scan_kernel, ctx_out=ctx_out),
        grid=(B, ML_HEADS),
        in_specs=stream_specs(T, lat_names) + stream_specs(Tc, ctx_names) + [head_vec, head_vec],
        out_specs=out_specs,
        out_shape=out_shape,
        scratch_shapes=[pltpu.VMEM((T, dh), F32), pltpu.VMEM((Tc, dh), F32), pltpu.VMEM((dh, dh), F32)],
        compiler_params=_cparams(2),
        name="mlstm_scan",
    )(*[lat[nm] for nm in lat_names], *[ctx[nm] for nm in ctx_names], head_norm, skip)
    return (outs[0], outs[1]) if ctx_out else (outs[0], None)


def _swap_pairs(a):
    g = a.reshape(a.shape[:-1] + (2, 2, QK_ROPE // 4))
    return jnp.flip(g, axis=-2).reshape(a.shape)


def _prep_mla_weights(w_in, q_norm, kv_norm, w_uq, w_ukv, w_out):
    kr0 = Q_LORA + KV_LORA
    kr = w_in[:, kr0:kr0 + QK_ROPE]
    krs = _swap_pairs(kr)
    w_in_p = jnp.concatenate([w_in[:, :kr0], kr, krs, kr, krs, w_in[:, kr0 + QK_ROPE:]], axis=1)
    uq = w_uq.reshape(Q_LORA, MLA_HEADS, QK_HEAD)
    rope = uq[..., QK_NOPE:]
    uq_p = jnp.concatenate([uq[..., :QK_NOPE], rope, _swap_pairs(rope)], axis=-1).reshape(Q_LORA, MLA_QW)
    ukv = w_ukv.reshape(KV_LORA, MLA_HEADS, QK_NOPE + V_HEAD)
    uk_p = jnp.concatenate([ukv[..., :QK_NOPE], jnp.zeros((KV_LORA, MLA_HEADS, HEAD_PAD - QK_NOPE), w_ukv.dtype)],
                           axis=-1).reshape(KV_LORA, MLA_QW)
    uv = ukv[..., QK_NOPE:].reshape(KV_LORA, MLA_WIDTH)
    return {"w_in": w_in_p.astype(BF16), "q_norm": q_norm.reshape(1, -1), "kv_norm": kv_norm.reshape(1, -1),
            "w_uq": uq_p.astype(BF16), "w_ukv": jnp.concatenate([uk_p, uv], axis=1).astype(BF16),
            "w_out": w_out.astype(BF16)}


def _rope_tables(n_tokens, rotate):
    if rotate:
        rows = n_tokens // GRID_W
        row = jnp.repeat(jnp.arange(rows, dtype=jnp.int32), GRID_W).astype(F32)
        col = jnp.tile(jnp.arange(GRID_W, dtype=jnp.int32), rows).astype(F32)
        qd = QK_ROPE // 4
        inv = ROPE_THETA ** (-jnp.arange(qd, dtype=F32) / qd)
        ang = jnp.stack([row[:, None] * inv, col[:, None] * inv], axis=1)
        cos, sin = jnp.cos(ang), jnp.sin(ang)
    else:
        cos = jnp.ones((n_tokens, 2, QK_ROPE // 4), F32)
        sin = jnp.zeros((n_tokens, 2, QK_ROPE // 4), F32)
    cf = jnp.stack([cos, cos], axis=2).reshape(n_tokens, QK_ROPE)
    sf = jnp.stack([-sin, sin], axis=2).reshape(n_tokens, QK_ROPE)
    tabq = ATTN_SCALE * jnp.concatenate([jnp.ones((n_tokens, QK_NOPE), F32), cf, sf], axis=1)
    tabk = jnp.concatenate([cf, sf, cf, sf], axis=1)
    return tabq, tabk


def _dense_blockdiag(w):
    per = MXU_DIM // QKV_BLOCK
    wt = w.reshape(-1, per, QKV_BLOCK, QKV_BLOCK)
    eye = jnp.eye(per, dtype=w.dtype)
    dense = jnp.einsum('jncd,nm->jncmd', wt, eye)
    return dense.reshape(-1, MXU_DIM, MXU_DIM).astype(BF16)


def _prep_ml_weights(w_in, conv_w, conv_b, w_q, w_k, w_v, w_if, b_if, head_norm, skip, w_out, tm):
    H = ML_HEADS
    cols = jnp.stack([w_if[0, :, :H], w_if[0, :, H:], w_if[1, :, :H], w_if[1, :, H:]], axis=-1)
    cols = jnp.concatenate([cols, jnp.zeros(cols.shape[:2] + (GATE_LANES - 4,), cols.dtype)], axis=-1)
    w_if_p = jnp.concatenate([cols.reshape(cols.shape[0], H * GATE_LANES),
                              jnp.zeros((cols.shape[0], LANES - H * GATE_LANES), cols.dtype)], axis=1)
    bcols = jnp.stack([b_if[0, :H], b_if[0, H:], b_if[1, :H], b_if[1, H:]], axis=-1)
    bcols = jnp.concatenate([bcols, jnp.zeros((H, GATE_LANES - 4), bcols.dtype)], axis=-1).reshape(1, -1)
    b_if_p = jnp.concatenate([bcols, jnp.zeros((1, LANES - H * GATE_LANES), bcols.dtype)], axis=1)
    t = jnp.arange(tm)
    same = (t[:, None] // SCAN_CHUNK) == (t[None, :] // SCAN_CHUNK)
    tri_f = (same & (t[None, :] <= t[:, None])).astype(BF16)
    tri_b = (same & (t[None, :] >= t[:, None])).astype(BF16)
    return {"w_in": w_in.astype(BF16), "conv_w": conv_w, "conv_b": conv_b.reshape(1, -1),
            "w_q": _dense_blockdiag(w_q), "w_k": _dense_blockdiag(w_k), "w_v": _dense_blockdiag(w_v),
            "w_if": w_if_p.astype(BF16), "b_if": b_if_p, "tri_f": tri_f, "tri_b": tri_b,
            "head_norm": head_norm.reshape(1, -1), "skip": skip.reshape(1, -1), "w_out": w_out.astype(BF16)}


def _row_tile(t, want):
    tm = min(t, want)
    assert t % tm == 0, (t, tm)
    return tm


def kernel(x, c, ctx, c_ctx, ada_w, ada_b, norm_pre, norm_post, mla_w_in, mla_q_norm, mla_kv_norm, mla_w_uq,
           mla_w_ukv, mla_w_out, ml_w_in, ml_conv_w, ml_conv_b, ml_w_q, ml_w_k, ml_w_v, ml_w_if, ml_b_if,
           ml_head_norm, ml_skip, ml_w_out):
    B, T, D = x.shape
    Tc = ctx.shape[1]
    assert D == D_MODEL and T % GRID_W == 0 and T % SCAN_CHUNK == 0 and Tc % SCAN_CHUNK == 0
    tm_l = _row_tile(T, 512)
    tm_c = _row_tile(Tc, 256)
    tf_l = _row_tile(T, 256)
    tf_c = _row_tile(Tc, 256)
    tq_l = _row_tile(T, 256)
    tq_c = _row_tile(Tc, 256)

    n_rows = -(-(B + 1) // 8) * 8
    cc = jnp.concatenate([c, c_ctx[None, :], jnp.zeros((n_rows - B - 1, D), F32)], axis=0)
    mods = _modulation(cc, ada_w, ada_b)

    tabq_l, tabk_l = _rope_tables(T, True)
    tabq_c, tabk_c = _rope_tables(Tc, False)

    for i in range(DEPTH):
        need_ctx_out = i < DEPTH - 1
        j = i // 2
        m_l = mods[i, :B]
        m_c = jnp.broadcast_to(mods[i, B:B + 1], (B, 3 * D))
        mod_l = jnp.stack([m_l[:, :D], m_l[:, D:2 * D]], axis=1)
        mod_c = jnp.stack([m_c[:, :D], m_c[:, D:2 * D]], axis=1)
        gate_l = m_l[:, None, 2 * D:]
        gate_c = m_c[:, None, 2 * D:]
        npre = norm_pre[i].reshape(1, D)
        npost = norm_post[i].reshape(1, D)
        if i % 2 == 0:
            w = _prep_mla_weights(mla_w_in[j], mla_q_norm[j], mla_kv_norm[j], mla_w_uq[j], mla_w_ukv[j],
                                  mla_w_out[j])
            ql, kl, vl, sgl = _mla_project(x, mod_l, npre, w, tabq_l, tabk_l, tm_l)
            qc, kc, vc, sgc = _mla_project(ctx, mod_c, npre, w, tabq_c, tabk_c, tm_c)
            al = _attention(ql, [kc, kl], [vc, vl], sgl, tq_l)
            ac = _attention(qc, [kc], [vc], sgc, tq_c) if need_ctx_out else None
            w_out = w["w_out"]
        else:
            w = _prep_ml_weights(ml_w_in[j], ml_conv_w[j], ml_conv_b[j], ml_w_q[j], ml_w_k[j], ml_w_v[j],
                                 ml_w_if[j], ml_b_if[j], ml_head_norm[j], ml_skip[j], ml_w_out[j], tf_l)
            assert tf_l == tf_c
            streams = []
            for xs, mod, tm, tf in ((x, mod_l, tm_l, tf_l), (ctx, mod_c, tm_c, tf_c)):
                xm, sz, og = _ml_inproject(xs, mod, npre, w["w_in"], tm)
                q, k, v, xc, gc, gr = _ml_features(xm, w, tf)
                streams.append({"q": q, "k": k, "v": v, "gc": gc, "gr": gr, "og": og, "xc": xc, "sz": sz})
            al, ac = _ml_scan(streams[0], streams[1], w["head_norm"], w["skip"], need_ctx_out)
            w_out = w["w_out"]
        x = _out_project(al, w_out, x, gate_l, npost, tm_l)
        if need_ctx_out:
            ctx = _out_project(ac, w_out, ctx, gate_c, npost, tm_c)
    return x
```

```python
import functools

import jax
import jax.numpy as jnp
from jax import lax
from jax.experimental import pallas as pl
from jax.experimental.pallas import tpu as pltpu

F32 = jnp.float32
BF16 = jnp.bfloat16

D_MODEL = 1024
DEPTH = 4
GRID_W = 64
EPS = 1e-6
ROPE_THETA = 10000.0
MLA_HEADS = 16
QK_NOPE = 64
QK_ROPE = 32
V_HEAD = 64
Q_LORA = 256
KV_LORA = 256
QK_HEAD = QK_NOPE + QK_ROPE
MLA_WIDTH = MLA_HEADS * V_HEAD
ATTN_SCALE = QK_HEAD ** -0.5
ML_INNER = 2 * D_MODEL
ML_HEADS = 8
ML_HEAD_DIM = ML_INNER // ML_HEADS
QKV_BLOCK = 4
CONV_K = 5

LANES = 128
MXU_DIM = 256
VMEM_LIMIT = 56 * 1024 * 1024

HEAD_PAD = LANES
MLA_QW = MLA_HEADS * HEAD_PAD
SCAN_CHUNK = MXU_DIM
HALO = 16
GATE_LANES = 8

NT_DIMS = (((1,), (1,)), ((), ()))
TN_DIMS = (((0,), (0,)), ((), ()))


def _cparams(n_axes):
    return pltpu.CompilerParams(dimension_semantics=("parallel",) * n_axes,
                                vmem_limit_bytes=VMEM_LIMIT)


def _const_spec(shape):
    nd = len(shape)
    return pl.BlockSpec(shape, lambda *_: (0,) * nd)


def _rms(x, g):
    r = lax.rsqrt(jnp.mean(x * x, axis=-1, keepdims=True) + EPS)
    return (x * r) * g


def _silu(x):
    return x * jax.nn.sigmoid(x)


def _split3(x):
    x1 = x.astype(BF16)
    r1 = x - x1.astype(F32)
    x2 = r1.astype(BF16)
    x3 = (r1 - x2.astype(F32)).astype(BF16)
    return x1, x2, x3


def _dot(a, b):
    return jnp.dot(a, b, preferred_element_type=F32)


def _mod_kernel(c_ref, w_ref, b_ref, o_ref):
    a1, a2, a3 = _split3(_silu(c_ref[...]))
    w = w_ref[0]
    w1 = w.astype(BF16)
    w2 = (w - w1.astype(F32)).astype(BF16)
    acc = _dot(a1, w1) + (_dot(a1, w2) + _dot(a2, w1)) + (_dot(a2, w2) + _dot(a3, w1))
    o_ref[0] = acc + b_ref[0]


def _modulation(cc, ada_w, ada_b):
    rows = cc.shape[0]
    nblk = ada_w.shape[2] // D_MODEL
    return pl.pallas_call(
        _mod_kernel,
        grid=(DEPTH, nblk),
        in_specs=[pl.BlockSpec((rows, D_MODEL), lambda i, j: (0, 0)),
                  pl.BlockSpec((1, D_MODEL, D_MODEL), lambda i, j: (i, 0, j)),
                  pl.BlockSpec((1, 1, D_MODEL), lambda i, j: (i, 0, j))],
        out_specs=pl.BlockSpec((1, rows, D_MODEL), lambda i, j: (i, 0, j)),
        out_shape=jax.ShapeDtypeStruct((DEPTH, rows, ada_w.shape[2]), F32),
        compiler_params=_cparams(2),
        name="modulation",
    )(cc, ada_w, ada_b.reshape(DEPTH, 1, -1))


def _prenorm(x_ref, mod_ref, npre_ref):
    h = _rms(x_ref[0], npre_ref[...]) * (1.0 + mod_ref[0, 1:2, :]) + mod_ref[0, 0:1, :]
    return h.astype(BF16)


def _mla_proj_kernel(x_ref, mod_ref, npre_ref, win_ref, qn_ref, kvn_ref, wuq_ref, wukv_ref,
                     tabq_ref, tabk_ref, q_ref, k_ref, v_ref, sg_ref):
    h = _prenorm(x_ref, mod_ref, npre_ref)
    u = _dot(h, win_ref[...])
    cq = u[:, :Q_LORA]
    ckv = u[:, Q_LORA:Q_LORA + KV_LORA]
    krp = u[:, Q_LORA + KV_LORA:Q_LORA + KV_LORA + LANES]
    gate = u[:, Q_LORA + KV_LORA + LANES:]
    q = _dot(_rms(cq, qn_ref[...]).astype(BF16), wuq_ref[...])
    kv = _dot(_rms(ckv, kvn_ref[...]).astype(BF16), wukv_ref[...])
    t = krp * tabk_ref[...]
    r = t + pltpu.roll(t, QK_ROPE, 1)
    lane = lax.broadcasted_iota(jnp.int32, r.shape, 1)
    rk = jnp.where(lane >= QK_NOPE, r, 0.0)
    tabq = tabq_ref[...]
    for hh in range(MLA_HEADS):
        sl = slice(HEAD_PAD * hh, HEAD_PAD * (hh + 1))
        q_ref[0, :, sl] = (q[:, sl] * tabq).astype(BF16)
        k_ref[0, :, sl] = (kv[:, sl] + rk).astype(BF16)
    v_ref[0] = kv[:, MLA_QW:].astype(BF16)
    sg_ref[0] = _silu(gate).astype(BF16)


def _mla_project(x, mod, npre, w, tabq, tabk, tm):
    B, T, D = x.shape
    nin = w["w_in"].shape[1]
    nkv = w["w_ukv"].shape[1]
    row = lambda width: pl.BlockSpec((1, tm, width), lambda b, i: (b, i, 0))
    return pl.pallas_call(
        _mla_proj_kernel,
        grid=(B, T // tm),
        in_specs=[row(D),
                  pl.BlockSpec((1, 2, D), lambda b, i: (b, 0, 0)),
                  _const_spec((1, D)),
                  _const_spec((D, nin)),
                  _const_spec((1, Q_LORA)),
                  _const_spec((1, KV_LORA)),
                  _const_spec((Q_LORA, MLA_QW)),
                  _const_spec((KV_LORA, nkv)),
                  pl.BlockSpec((tm, LANES), lambda b, i: (i, 0)),
                  pl.BlockSpec((tm, LANES), lambda b, i: (i, 0))],
        out_specs=[row(MLA_QW), row(MLA_QW), row(MLA_WIDTH), row(MLA_WIDTH)],
        out_shape=[jax.ShapeDtypeStruct((B, T, MLA_QW), BF16),
                   jax.ShapeDtypeStruct((B, T, MLA_QW), BF16),
                   jax.ShapeDtypeStruct((B, T, MLA_WIDTH), BF16),
                   jax.ShapeDtypeStruct((B, T, MLA_WIDTH), BF16)],
        compiler_params=_cparams(2),
        name="mla_project",
    )(x, mod, npre, w["w_in"], w["q_norm"], w["kv_norm"], w["w_uq"], w["w_ukv"], tabq, tabk)


def _attn_kernel(*refs, n_seg, tq):
    q_ref = refs[0]
    k_refs = refs[1:1 + n_seg]
    v_refs = refs[1 + n_seg:1 + 2 * n_seg]
    sg_ref = refs[1 + 2 * n_seg]
    o_ref = refs[2 + 2 * n_seg]
    n_q = q_ref.shape[1] // tq
    lane = lax.broadcasted_iota(jnp.int32, (tq, LANES), 1)

    def body(i, carry):
        rows = pl.ds(pl.multiple_of(i * tq, tq), tq)
        outs = []
        for hh in range(2):
            sl = slice(HEAD_PAD * hh, HEAD_PAD * (hh + 1))
            qb = q_ref[0, rows, sl]
            ss = [lax.dot_general(qb, k[0, :, sl], NT_DIMS, preferred_element_type=F32) for k in k_refs]
            m = functools.reduce(jnp.maximum, [jnp.max(s, axis=-1, keepdims=True) for s in ss])
            ps = [jnp.exp(s - m) for s in ss]
            l = functools.reduce(jnp.add, [jnp.sum(p, axis=-1, keepdims=True) for p in ps])
            o = functools.reduce(jnp.add, [_dot(p.astype(BF16), v[0]) for p, v in zip(ps, v_refs)])
            outs.append(o / l)
        o = jnp.where(lane < V_HEAD, outs[0], outs[1])
        o_ref[0, rows, :] = (o * sg_ref[0, rows, :].astype(F32)).astype(BF16)
        return carry

    lax.fori_loop(0, n_q, body, 0)


def _attention(q, ks, vs, sg, tq):
    B, Tq, _ = q.shape
    n_seg = len(ks)
    pair = lambda t, width: pl.BlockSpec((1, t, width), lambda b, p: (b, 0, p))
    return pl.pallas_call(
        functools.partial(_attn_kernel, n_seg=n_seg, tq=tq),
        grid=(B, MLA_HEADS // 2),
        in_specs=([pair(Tq, 2 * HEAD_PAD)]
                  + [pair(k.shape[1], 2 * HEAD_PAD) for k in ks]
                  + [pair(v.shape[1], 2 * V_HEAD) for v in vs]
                  + [pair(Tq, 2 * V_HEAD)]),
        out_specs=pair(Tq, 2 * V_HEAD),
        out_shape=jax.ShapeDtypeStruct((B, Tq, MLA_WIDTH), BF16),
        compiler_params=_cparams(2),
        name="mla_attention",
    )(q, *ks, *vs, sg)


def _out_kernel(a_ref, w_ref, x_ref, g_ref, npost_ref, o_ref):
    y = _dot(a_ref[0], w_ref[...])
    o_ref[0] = x_ref[0] + g_ref[0] * _rms(y, npost_ref[...])


def _out_project(a, w_out, x, gate, npost, tm):
    B, T, K = a.shape
    D = x.shape[-1]
    return pl.pallas_call(
        _out_kernel,
        grid=(B, T // tm),
        in_specs=[pl.BlockSpec((1, tm, K), lambda b, i: (b, i, 0)),
                  _const_spec((K, D)),
                  pl.BlockSpec((1, tm, D), lambda b, i: (b, i, 0)),
                  pl.BlockSpec((1, 1, D), lambda b, i: (b, 0, 0)),
                  _const_spec((1, D))],
        out_specs=pl.BlockSpec((1, tm, D), lambda b, i: (b, i, 0)),
        out_shape=jax.ShapeDtypeStruct(x.shape, F32),
        compiler_params=_cparams(2),
        name="out_project",
    )(a, w_out, x, gate, npost)


def _ml_inproj_kernel(x_ref, mod_ref, npre_ref, w_ref, xm_ref, sz_ref, og_ref):
    h = _prenorm(x_ref, mod_ref, npre_ref)
    xm_ref[0] = _dot(h, w_ref[:, :ML_INNER]).astype(BF16)
    sz_ref[0] = _silu(_dot(h, w_ref[:, ML_INNER:2 * ML_INNER])).astype(BF16)
    og_ref[0] = jax.nn.sigmoid(_dot(h, w_ref[:, 2 * ML_INNER:])).astype(BF16)


def _ml_inproject(x, mod, npre, w_in, tm):
    B, T, D = x.shape
    row = lambda width: pl.BlockSpec((1, tm, width), lambda b, i: (b, i, 0))
    return pl.pallas_call(
        _ml_inproj_kernel,
        grid=(B, T // tm),
        in_specs=[row(D),
                  pl.BlockSpec((1, 2, D), lambda b, i: (b, 0, 0)),
                  _const_spec((1, D)),
                  _const_spec((D, 3 * ML_INNER))],
        out_specs=[row(ML_INNER)] * 3,
        out_shape=[jax.ShapeDtypeStruct((B, T, ML_INNER), BF16)] * 3,
        compiler_params=_cparams(2),
        name="mlstm_inproject",
    )(x, mod, npre, w_in)


def _ml_feat_kernel(xm_ref, xp_ref, xn_ref, cw_ref, cb_ref, wq_ref, wk_ref, wv_ref, wvt_ref, wif_ref, bif_ref,
                    trif_ref, trib_ref, q_ref, k_ref, vt_ref, xc_ref, gc_ref, gr_ref, xe_ref, *, tm):
    i = pl.program_id(1)
    last = pl.num_programs(1) - 1
    pad = CONV_K // 2
    xm = xm_ref[0]
    half = HALO // 2
    xe_ref[0:half, :] = jnp.where(i > 0, xp_ref[0, half:, :].astype(F32), 0.0)
    xe_ref[half:half + tm, :] = xm.astype(F32)
    xe_ref[half + tm:, :] = jnp.where(i < last, xn_ref[0, :half, :].astype(F32), 0.0)
    acc = cb_ref[...] + cw_ref[0:1, :] * xe_ref[half - pad:half - pad + tm, :]
    for j in range(1, CONV_K):
        acc = acc + cw_ref[j:j + 1, :] * xe_ref[half - pad + j:half - pad + j + tm, :]
    xc = _silu(acc)
    xcb = xc.astype(BF16)
    xc_ref[0] = xcb

    n_tiles = ML_INNER // MXU_DIM
    g = jnp.zeros((tm, LANES), F32) + bif_ref[...]
    for j in range(n_tiles):
        sl = slice(MXU_DIM * j, MXU_DIM * (j + 1))
        qj = _dot(xcb[:, sl], wq_ref[j]).astype(BF16)
        kj = _dot(xcb[:, sl], wk_ref[j])
        vj = _dot(xm[:, sl], wv_ref[j]).astype(BF16)
        q_ref[0, :, sl] = qj
        k_ref[0, :, sl] = (kj * (ML_HEAD_DIM ** -0.5)).astype(BF16)
        vt_ref[0, sl, :] = lax.dot_general(wvt_ref[j], xm[:, sl], NT_DIMS,
                                           preferred_element_type=F32).astype(BF16)
        g = g + _dot(qj, wif_ref[sl, :])
        g = g + _dot(kj.astype(BF16), wif_ref[ML_INNER + MXU_DIM * j:ML_INNER + MXU_DIM * (j + 1), :])
        g = g + _dot(vj, wif_ref[2 * ML_INNER + MXU_DIM * j:2 * ML_INNER + MXU_DIM * (j + 1), :])

    lane = lax.broadcasted_iota(jnp.int32, g.shape, 1)
    sub = lane % GATE_LANES
    is_f_fwd = sub == 1
    is_f_bwd = sub == 3
    logsig = jnp.minimum(g, 0.0) - jnp.log(1.0 + jnp.exp(-jnp.abs(g)))
    gs = jnp.where(is_f_fwd | is_f_bwd, logsig, g)
    g1, g2, g3 = _split3(gs)
    cum_f = _dot(trif_ref[...], g1) + _dot(trif_ref[...], g2) + _dot(trif_ref[...], g3)
    cum_b = _dot(trib_ref[...], g1) + _dot(trib_ref[...], g2) + _dot(trib_ref[...], g3)
    out = jnp.where(is_f_fwd, cum_f, jnp.where(is_f_bwd, cum_b, gs))
    gr_ref[0] = out.T[:ML_HEADS * GATE_LANES, :]
    diff = out - pltpu.roll(out, LANES - 1, 1)
    for h in range(ML_HEADS):
        gc_ref[0, h] = diff[:, GATE_LANES * h:GATE_LANES * (h + 1)]


def _ml_features(xm, w, tm):
    B, T, C = xm.shape
    nh = tm // HALO
    n_halo = T // HALO
    row = pl.BlockSpec((1, tm, C), lambda b, i: (b, i, 0))
    n_tiles = C // MXU_DIM
    tile_w = _const_spec((n_tiles, MXU_DIM, MXU_DIM))
    return pl.pallas_call(
        functools.partial(_ml_feat_kernel, tm=tm),
        grid=(B, T // tm),
        in_specs=[row,
                  pl.BlockSpec((1, HALO, C), lambda b, i: (b, jnp.maximum(i * nh - 1, 0), 0)),
                  pl.BlockSpec((1, HALO, C), lambda b, i: (b, jnp.minimum((i + 1) * nh, n_halo - 1), 0)),
                  _const_spec((CONV_K, C)),
                  _const_spec((1, C)),
                  tile_w, tile_w, tile_w, tile_w,
                  _const_spec((3 * C, LANES)),
                  _const_spec((1, LANES)),
                  _const_spec((tm, tm)),
                  _const_spec((tm, tm))],
        out_specs=[row, row,
                   pl.BlockSpec((1, C, tm), lambda b, i: (b, 0, i)),
                   row,
                   pl.BlockSpec((1, ML_HEADS, tm, GATE_LANES), lambda b, i: (b, 0, i, 0)),
                   pl.BlockSpec((1, ML_HEADS * GATE_LANES, tm), lambda b, i: (b, 0, i))],
        out_shape=[jax.ShapeDtypeStruct((B, T, C), BF16),
                   jax.ShapeDtypeStruct((B, T, C), BF16),
                   jax.ShapeDtypeStruct((B, C, T), BF16),
                   jax.ShapeDtypeStruct((B, T, C), BF16),
                   jax.ShapeDtypeStruct((B, ML_HEADS, T, GATE_LANES), F32),
                   jax.ShapeDtypeStruct((B, ML_HEADS * GATE_LANES, T), F32)],
        scratch_shapes=[pltpu.VMEM((tm + HALO, C), F32)],
        compiler_params=_cparams(2),
        name="mlstm_features",
    )(xm, xm, xm, w["conv_w"], w["conv_b"], w["w_q"], w["w_k"], w["w_v"], w["w_vt"], w["w_if"], w["b_if"],
      w["tri_f"], w["tri_b"])


STATE_ROWS = ML_HEAD_DIM + 16


def _scan_chunk(q_ref, k_ref, vt_ref, gc_ref, gr_ref, ht_ref, cn_ref, m, r0, d, with_out):
    L = SCAN_CHUNK
    dh = ML_HEAD_DIM
    rows = pl.ds(r0, L)
    kk = k_ref[0, rows, :]
    vt = vt_ref[0, :, rows]
    ig_r = gr_ref[0, 2 * d:2 * d + 1, rows]
    b_r = gr_ref[0, 2 * d + 1:2 * d + 2, rows]
    b_end = b_r[:, L - 1:L] if d == 0 else b_r[:, 0:1]
    w_r = b_end - b_r + ig_r
    m_new = jnp.maximum(b_end + m, jnp.max(w_r, axis=-1, keepdims=True))
    a_r = jnp.exp(w_r - m_new)
    decay = jnp.exp(b_end + m - m_new)
    cn = cn_ref[d]
    if with_out:
        qq = q_ref[0, rows, :]
        u_c = gc_ref[0, 0, rows, 2 * d:2 * d + 1]
        si = lax.broadcasted_iota(jnp.int32, (L, L), 0)
        ti = lax.broadcasted_iota(jnp.int32, (L, L), 1)
        mask = (si <= ti) if d == 0 else (si >= ti)
        log_d = jnp.where(mask, u_c + b_r, -jnp.inf)
        m_t = jnp.maximum(b_r + m, jnp.max(log_d, axis=0, keepdims=True))
        inter = jnp.exp(b_r + m - m_t)
        s = lax.dot_general(kk, qq, NT_DIMS, preferred_element_type=F32) * jnp.exp(log_d - m_t)
        cq = lax.dot_general(cn.astype(BF16), qq, NT_DIMS, preferred_element_type=F32)
        num = _dot(vt, s.astype(BF16)) + inter * cq[:dh]
        den = jnp.sum(s, axis=0, keepdims=True) + inter * cq[dh:dh + 1]
        ht_ref[:, rows] += num * (1.0 / jnp.maximum(jnp.abs(den), jnp.exp(-m_t)))
    sub = lax.broadcasted_iota(jnp.int32, (STATE_ROWS - dh, L), 0)
    va = jnp.concatenate([(vt.astype(F32) * a_r).astype(BF16),
                          jnp.where(sub == 0, a_r, 0.0).astype(BF16)], axis=0)
    cn_ref[d] = decay * cn + _dot(va, kk)
    return m_new


def _scan_finish(ht_ref, og_ref, xc_ref, sz_ref, hn_ref, skip_ref, y_ref):
    L = SCAN_CHUNK

    def body(i, carry):
        rows = pl.ds(pl.multiple_of(i * L, L), L)
        hh = og_ref[0, rows, :].astype(F32) * ht_ref[:, rows].T
        hn = hh * lax.rsqrt(jnp.mean(hh * hh, axis=-1, keepdims=True) + EPS)
        y = (hn * hn_ref[...] + skip_ref[...] * xc_ref[0, rows, :].astype(F32)) * sz_ref[0, rows, :].astype(F32)
        y_ref[0, rows, :] = y.astype(BF16)
        return carry

    lax.fori_loop(0, ht_ref.shape[1] // L, body, 0)


def _ml_scan_kernel(*refs, ctx_out):
    (ql, kl, vtl, gcl, grl, ogl, xcl, szl, qc, kc, vtc, gcc, grc) = refs[:13]
    pos = 13
    if ctx_out:
        ogc, xcc, szc = refs[pos:pos + 3]
        pos += 3
    hn_ref, skip_ref = refs[pos:pos + 2]
    pos += 2
    yl_ref = refs[pos]
    pos += 1
    if ctx_out:
        yc_ref = refs[pos]
        pos += 1
    hl_ref, hc_ref, cn_ref = refs[pos:pos + 3]
    L = SCAN_CHUNK
    n_l = ql.shape[1] // L
    n_c = kc.shape[1] // L

    cn_ref[...] = jnp.zeros_like(cn_ref)
    hl_ref[...] = jnp.zeros_like(hl_ref)
    if ctx_out:
        hc_ref[...] = jnp.zeros_like(hc_ref)
    m_f = jnp.zeros((1, 1), F32)
    m_b = jnp.zeros((1, 1), F32)
    for c in range(n_c):
        m_f = _scan_chunk(qc, kc, vtc, gcc, grc, hc_ref, cn_ref, m_f, c * L, 0, ctx_out)
        m_b = _scan_chunk(qc, kc, vtc, gcc, grc, hc_ref, cn_ref, m_b, (n_c - 1 - c) * L, 1, ctx_out)

    def body(i, carry):
        m_f = _scan_chunk(ql, kl, vtl, gcl, grl, hl_ref, cn_ref, carry[0], pl.multiple_of(i * L, L), 0, True)
        m_b = _scan_chunk(ql, kl, vtl, gcl, grl, hl_ref, cn_ref, carry[1],
                          pl.multiple_of((n_l - 1 - i) * L, L), 1, True)
        return m_f, m_b

    lax.fori_loop(0, n_l, body, (m_f, m_b))

    _scan_finish(hl_ref, ogl, xcl, szl, hn_ref, skip_ref, yl_ref)
    if ctx_out:
        _scan_finish(hc_ref, ogc, xcc, szc, hn_ref, skip_ref, yc_ref)


def _ml_scan(lat, ctx, head_norm, skip, ctx_out):
    B, T, C = lat["q"].shape
    Tc = ctx["q"].shape[1]
    dh = ML_HEAD_DIM

    def stream_specs(t, names):
        specs = []
        for nm in names:
            if nm == "gc":
                specs.append(pl.BlockSpec((1, 1, t, GATE_LANES), lambda b, h: (b, h, 0, 0)))
            elif nm == "gr":
                specs.append(pl.BlockSpec((1, GATE_LANES, t), lambda b, h: (b, h, 0)))
            elif nm == "vt":
                specs.append(pl.BlockSpec((1, dh, t), lambda b, h: (b, h, 0)))
            else:
                specs.append(pl.BlockSpec((1, t, dh), lambda b, h: (b, 0, h)))
        return specs

    lat_names = ["q", "k", "vt", "gc", "gr", "og", "xc", "sz"]
    ctx_names = ["q", "k", "vt", "gc", "gr"] + (["og", "xc", "sz"] if ctx_out else [])
    head_vec = pl.BlockSpec((1, dh), lambda b, h: (0, h))
    out_specs = [pl.BlockSpec((1, T, dh), lambda b, h: (b, 0, h))]
    out_shape = [jax.ShapeDtypeStruct((B, T, C), BF16)]
    if ctx_out:
        out_specs.append(pl.BlockSpec((1, Tc, dh), lambda b, h: (b, 0, h)))
        out_shape.append(jax.ShapeDtypeStruct((B, Tc, C), BF16))
    outs = pl.pallas_call(
        functools.partial(_ml_scan_kernel, ctx_out=ctx_out),
        grid=(B, ML_HEADS),
        in_specs=stream_specs(T, lat_names) + stream_specs(Tc, ctx_names) + [head_vec, head_vec],
        out_specs=out_specs,
        out_shape=out_shape,
        scratch_shapes=[pltpu.VMEM((dh, T), F32), pltpu.VMEM((dh, Tc), F32),
                        pltpu.VMEM((2, STATE_ROWS, dh), F32)],
        compiler_params=_cparams(2),
        name="mlstm_scan",
    )(*[lat[nm] for nm in lat_names], *[ctx[nm] for nm in ctx_names], head_norm, skip)
    return (outs[0], outs[1]) if ctx_out else (outs[0], None)


def _swap_pairs(a):
    g = a.reshape(a.shape[:-1] + (2, 2, QK_ROPE // 4))
    return jnp.flip(g, axis=-2).reshape(a.shape)


def _prep_mla_weights(w_in, q_norm, kv_norm, w_uq, w_ukv, w_out):
    kr0 = Q_LORA + KV_LORA
    kr = w_in[:, kr0:kr0 + QK_ROPE]
    krs = _swap_pairs(kr)
    w_in_p = jnp.concatenate([w_in[:, :kr0], kr, krs, kr, krs, w_in[:, kr0 + QK_ROPE:]], axis=1)
    uq = w_uq.reshape(Q_LORA, MLA_HEADS, QK_HEAD)
    rope = uq[..., QK_NOPE:]
    uq_p = jnp.concatenate([uq[..., :QK_NOPE], rope, _swap_pairs(rope)], axis=-1).reshape(Q_LORA, MLA_QW)
    ukv = w_ukv.reshape(KV_LORA, MLA_HEADS, QK_NOPE + V_HEAD)
    uk_p = jnp.concatenate([ukv[..., :QK_NOPE], jnp.zeros((KV_LORA, MLA_HEADS, HEAD_PAD - QK_NOPE), w_ukv.dtype)],
                           axis=-1).reshape(KV_LORA, MLA_QW)
    uv = ukv[..., QK_NOPE:].reshape(KV_LORA, MLA_WIDTH)
    return {"w_in": w_in_p.astype(BF16), "q_norm": q_norm.reshape(1, -1), "kv_norm": kv_norm.reshape(1, -1),
            "w_uq": uq_p.astype(BF16), "w_ukv": jnp.concatenate([uk_p, uv], axis=1).astype(BF16),
            "w_out": w_out.astype(BF16)}


def _rope_tables(n_tokens, rotate):
    if rotate:
        rows = n_tokens // GRID_W
        row = jnp.repeat(jnp.arange(rows, dtype=jnp.int32), GRID_W).astype(F32)
        col = jnp.tile(jnp.arange(GRID_W, dtype=jnp.int32), rows).astype(F32)
        qd = QK_ROPE // 4
        inv = ROPE_THETA ** (-jnp.arange(qd, dtype=F32) / qd)
        ang = jnp.stack([row[:, None] * inv, col[:, None] * inv], axis=1)
        cos, sin = jnp.cos(ang), jnp.sin(ang)
    else:
        cos = jnp.ones((n_tokens, 2, QK_ROPE // 4), F32)
        sin = jnp.zeros((n_tokens, 2, QK_ROPE // 4), F32)
    cf = jnp.stack([cos, cos], axis=2).reshape(n_tokens, QK_ROPE)
    sf = jnp.stack([-sin, sin], axis=2).reshape(n_tokens, QK_ROPE)
    tabq = ATTN_SCALE * jnp.concatenate([jnp.ones((n_tokens, QK_NOPE), F32), cf, sf], axis=1)
    tabk = jnp.concatenate([cf, sf, cf, sf], axis=1)
    return tabq, tabk


def _dense_blockdiag(w):
    per = MXU_DIM // QKV_BLOCK
    wt = w.reshape(-1, per, QKV_BLOCK, QKV_BLOCK)
    eye = jnp.eye(per, dtype=w.dtype)
    dense = jnp.einsum('jncd,nm->jncmd', wt, eye)
    return dense.reshape(-1, MXU_DIM, MXU_DIM).astype(BF16)


def _prep_ml_weights(w_in, conv_w, conv_b, w_q, w_k, w_v, w_if, b_if, head_norm, skip, w_out, tm):
    H = ML_HEADS
    cols = jnp.stack([w_if[0, :, :H], w_if[0, :, H:], w_if[1, :, :H], w_if[1, :, H:]], axis=-1)
    cols = jnp.concatenate([cols, jnp.zeros(cols.shape[:2] + (GATE_LANES - 4,), cols.dtype)], axis=-1)
    w_if_p = jnp.concatenate([cols.reshape(cols.shape[0], H * GATE_LANES),
                              jnp.zeros((cols.shape[0], LANES - H * GATE_LANES), cols.dtype)], axis=1)
    bcols = jnp.stack([b_if[0, :H], b_if[0, H:], b_if[1, :H], b_if[1, H:]], axis=-1)
    bcols = jnp.concatenate([bcols, jnp.zeros((H, GATE_LANES - 4), bcols.dtype)], axis=-1).reshape(1, -1)
    b_if_p = jnp.concatenate([bcols, jnp.zeros((1, LANES - H * GATE_LANES), bcols.dtype)], axis=1)
    t = jnp.arange(tm)
    same = (t[:, None] // SCAN_CHUNK) == (t[None, :] // SCAN_CHUNK)
    tri_f = (same & (t[None, :] <= t[:, None])).astype(BF16)
    tri_b = (same & (t[None, :] >= t[:, None])).astype(BF16)
    return {"w_in": w_in.astype(BF16), "conv_w": conv_w, "conv_b": conv_b.reshape(1, -1),
            "w_q": _dense_blockdiag(w_q), "w_k": _dense_blockdiag(w_k), "w_v": _dense_blockdiag(w_v),
            "w_vt": jnp.swapaxes(_dense_blockdiag(w_v), 1, 2),
            "w_if": w_if_p.astype(BF16), "b_if": b_if_p, "tri_f": tri_f, "tri_b": tri_b,
            "head_norm": head_norm.reshape(1, -1), "skip": skip.reshape(1, -1), "w_out": w_out.astype(BF16)}


def _row_tile(t, want):
    tm = min(t, want)
    assert t % tm == 0, (t, tm)
    return tm


def kernel(x, c, ctx, c_ctx, ada_w, ada_b, norm_pre, norm_post, mla_w_in, mla_q_norm, mla_kv_norm, mla_w_uq,
           mla_w_ukv, mla_w_out, ml_w_in, ml_conv_w, ml_conv_b, ml_w_q, ml_w_k, ml_w_v, ml_w_if, ml_b_if,
           ml_head_norm, ml_skip, ml_w_out):
    B, T, D = x.shape
    Tc = ctx.shape[1]
    assert D == D_MODEL and T % GRID_W == 0 and T % SCAN_CHUNK == 0 and Tc % SCAN_CHUNK == 0
    tm_l = _row_tile(T, 512)
    tm_c = _row_tile(Tc, 256)
    tf_l = _row_tile(T, 256)
    tf_c = _row_tile(Tc, 256)
    tq_l = _row_tile(T, 256)
    tq_c = _row_tile(Tc, 256)

    n_rows = -(-(B + 1) // 8) * 8
    cc = jnp.concatenate([c, c_ctx[None, :], jnp.zeros((n_rows - B - 1, D), F32)], axis=0)
    mods = _modulation(cc, ada_w, ada_b)

    tabq_l, tabk_l = _rope_tables(T, True)
    tabq_c, tabk_c = _rope_tables(Tc, False)

    for i in range(DEPTH):
        need_ctx_out = i < DEPTH - 1
        j = i // 2
        m_l = mods[i, :B]
        m_c = jnp.broadcast_to(mods[i, B:B + 1], (B, 3 * D))
        mod_l = jnp.stack([m_l[:, :D], m_l[:, D:2 * D]], axis=1)
        mod_c = jnp.stack([m_c[:, :D], m_c[:, D:2 * D]], axis=1)
        gate_l = m_l[:, None, 2 * D:]
        gate_c = m_c[:, None, 2 * D:]
        npre = norm_pre[i].reshape(1, D)
        npost = norm_post[i].reshape(1, D)
        if i % 2 == 0:
            w = _prep_mla_weights(mla_w_in[j], mla_q_norm[j], mla_kv_norm[j], mla_w_uq[j], mla_w_ukv[j],
                                  mla_w_out[j])
            ql, kl, vl, sgl = _mla_project(x, mod_l, npre, w, tabq_l, tabk_l, tm_l)
            qc, kc, vc, sgc = _mla_project(ctx, mod_c, npre, w, tabq_c, tabk_c, tm_c)
            al = _attention(ql, [kc, kl], [vc, vl], sgl, tq_l)
            ac = _attention(qc, [kc], [vc], sgc, tq_c) if need_ctx_out else None
            w_out = w["w_out"]
        else:
            w = _prep_ml_weights(ml_w_in[j], ml_conv_w[j], ml_conv_b[j], ml_w_q[j], ml_w_k[j], ml_w_v[j],
                                 ml_w_if[j], ml_b_if[j], ml_head_norm[j], ml_skip[j], ml_w_out[j], tf_l)
            assert tf_l == tf_c
            streams = []
            for xs, mod, tm, tf in ((x, mod_l, tm_l, tf_l), (ctx, mod_c, tm_c, tf_c)):
                xm, sz, og = _ml_inproject(xs, mod, npre, w["w_in"], tm)
                q, k, vt, xc, gc, gr = _ml_features(xm, w, tf)
                streams.append({"q": q, "k": k, "vt": vt, "gc": gc, "gr": gr, "og": og, "xc": xc, "sz": sz})
            al, ac = _ml_scan(streams[0], streams[1], w["head_norm"], w["skip"], need_ctx_out)
            w_out = w["w_out"]
        x = _out_project(al, w_out, x, gate_l, npost, tm_l)
        if need_ctx_out:
            ctx = _out_project(ac, w_out, ctx, gate_c, npost, tm_c)
    return x
```

```python
import functools

import jax
import jax.numpy as jnp
from jax import lax
from jax.experimental import pallas as pl
from jax.experimental.pallas import tpu as pltpu

F32 = jnp.float32
BF16 = jnp.bfloat16

D_MODEL = 1024
DEPTH = 4
GRID_W = 64
EPS = 1e-6
ROPE_THETA = 10000.0
MLA_HEADS = 16
QK_NOPE = 64
QK_ROPE = 32
V_HEAD = 64
Q_LORA = 256
KV_LORA = 256
QK_HEAD = QK_NOPE + QK_ROPE
MLA_WIDTH = MLA_HEADS * V_HEAD
ATTN_SCALE = QK_HEAD ** -0.5
LOG2_E = 1.4426950408889634
ML_INNER = 2 * D_MODEL
ML_HEADS = 8
ML_HEAD_DIM = ML_INNER // ML_HEADS
QKV_BLOCK = 4
CONV_K = 5

LANES = 128
MXU_DIM = 256
VMEM_LIMIT = 56 * 1024 * 1024

HEAD_PAD = LANES
MLA_QW = MLA_HEADS * HEAD_PAD
SCAN_CHUNK = MXU_DIM
HALO = 16
GATE_LANES = 8
SCAN_HEADS = 2

NT_DIMS = (((1,), (1,)), ((), ()))
TN_DIMS = (((0,), (0,)), ((), ()))


def _cparams(n_axes):
    return pltpu.CompilerParams(dimension_semantics=("parallel",) * n_axes,
                                vmem_limit_bytes=VMEM_LIMIT)


def _const_spec(shape):
    nd = len(shape)
    return pl.BlockSpec(shape, lambda *_: (0,) * nd)


def _rms(x, g):
    r = lax.rsqrt(jnp.mean(x * x, axis=-1, keepdims=True) + EPS)
    return (x * r) * g


def _silu(x):
    return x * jax.nn.sigmoid(x)


def _split3(x):
    x1 = x.astype(BF16)
    r1 = x - x1.astype(F32)
    x2 = r1.astype(BF16)
    x3 = (r1 - x2.astype(F32)).astype(BF16)
    return x1, x2, x3


def _dot(a, b):
    return jnp.dot(a, b, preferred_element_type=F32)


def _mod_kernel(c_ref, w_ref, b_ref, o_ref):
    a1, a2, a3 = _split3(_silu(c_ref[...]))
    w = w_ref[0]
    w1 = w.astype(BF16)
    w2 = (w - w1.astype(F32)).astype(BF16)
    acc = _dot(a1, w1) + (_dot(a1, w2) + _dot(a2, w1)) + (_dot(a2, w2) + _dot(a3, w1))
    o_ref[0] = acc + b_ref[0]


def _modulation(cc, ada_w, ada_b):
    rows = cc.shape[0]
    nblk = ada_w.shape[2] // D_MODEL
    return pl.pallas_call(
        _mod_kernel,
        grid=(DEPTH, nblk),
        in_specs=[pl.BlockSpec((rows, D_MODEL), lambda i, j: (0, 0)),
                  pl.BlockSpec((1, D_MODEL, D_MODEL), lambda i, j: (i, 0, j)),
                  pl.BlockSpec((1, 1, D_MODEL), lambda i, j: (i, 0, j))],
        out_specs=pl.BlockSpec((1, rows, D_MODEL), lambda i, j: (i, 0, j)),
        out_shape=jax.ShapeDtypeStruct((DEPTH, rows, ada_w.shape[2]), F32),
        compiler_params=_cparams(2),
        name="modulation",
    )(cc, ada_w, ada_b.reshape(DEPTH, 1, -1))


def _prenorm(x_ref, mod_ref, npre_ref):
    h = _rms(x_ref[0], npre_ref[...]) * (1.0 + mod_ref[0, 1:2, :]) + mod_ref[0, 0:1, :]
    return h.astype(BF16)


def _mla_proj_kernel(x_ref, mod_ref, npre_ref, win_ref, qn_ref, kvn_ref, wuq_ref, wuk_ref, wuvt_ref,
                     tabq_ref, tabk_ref, q_ref, k_ref, vt_ref, sg_ref):
    h = _prenorm(x_ref, mod_ref, npre_ref)
    u = _dot(h, win_ref[...])
    cq = u[:, :Q_LORA]
    ckv = u[:, Q_LORA:Q_LORA + KV_LORA]
    krp = u[:, Q_LORA + KV_LORA:Q_LORA + KV_LORA + LANES]
    gate = u[:, Q_LORA + KV_LORA + LANES:]
    ckvn = _rms(ckv, kvn_ref[...]).astype(BF16)
    q = _dot(_rms(cq, qn_ref[...]).astype(BF16), wuq_ref[...])
    kn = _dot(ckvn, wuk_ref[...])
    vt_ref[0] = lax.dot_general(wuvt_ref[...], ckvn, NT_DIMS, preferred_element_type=F32).astype(BF16)
    t = krp * tabk_ref[...]
    r = t + pltpu.roll(t, QK_ROPE, 1)
    lane = lax.broadcasted_iota(jnp.int32, r.shape, 1)
    rk = jnp.where(lane >= QK_NOPE, r, 0.0)
    tabq = tabq_ref[...]
    for hh in range(MLA_HEADS):
        sl = slice(HEAD_PAD * hh, HEAD_PAD * (hh + 1))
        q_ref[0, :, sl] = (q[:, sl] * tabq).astype(BF16)
        k_ref[0, :, sl] = (kn[:, sl] + rk).astype(BF16)
    sg_ref[0] = _silu(gate).astype(BF16)


def _mla_project(x, mod, npre, w, tabq, tabk, tm):
    B, T, D = x.shape
    nin = w["w_in"].shape[1]
    row = lambda width: pl.BlockSpec((1, tm, width), lambda b, i: (b, i, 0))
    return pl.pallas_call(
        _mla_proj_kernel,
        grid=(B, T // tm),
        in_specs=[row(D),
                  pl.BlockSpec((1, 2, D), lambda b, i: (b, 0, 0)),
                  _const_spec((1, D)),
                  _const_spec((D, nin)),
                  _const_spec((1, Q_LORA)),
                  _const_spec((1, KV_LORA)),
                  _const_spec((Q_LORA, MLA_QW)),
                  _const_spec((KV_LORA, MLA_QW)),
                  _const_spec((MLA_WIDTH, KV_LORA)),
                  pl.BlockSpec((tm, LANES), lambda b, i: (i, 0)),
                  pl.BlockSpec((tm, LANES), lambda b, i: (i, 0))],
        out_specs=[row(MLA_QW), row(MLA_QW),
                   pl.BlockSpec((1, MLA_WIDTH, tm), lambda b, i: (b, 0, i)),
                   row(MLA_WIDTH)],
        out_shape=[jax.ShapeDtypeStruct((B, T, MLA_QW), BF16),
                   jax.ShapeDtypeStruct((B, T, MLA_QW), BF16),
                   jax.ShapeDtypeStruct((B, MLA_WIDTH, T), BF16),
                   jax.ShapeDtypeStruct((B, T, MLA_WIDTH), BF16)],
        compiler_params=_cparams(2),
        name="mla_project",
    )(x, mod, npre, w["w_in"], w["q_norm"], w["kv_norm"], w["w_uq"], w["w_uk"], w["w_uvt"], tabq, tabk)


VAUG_ROWS = V_HEAD + 16


def _attn_kernel(*refs, n_seg, tq):
    q_ref = refs[0]
    k_refs = refs[1:1 + n_seg]
    vt_refs = refs[1 + n_seg:1 + 2 * n_seg]
    sg_ref = refs[1 + 2 * n_seg]
    o_ref = refs[2 + 2 * n_seg]
    va_refs = refs[3 + 2 * n_seg:3 + 3 * n_seg]
    ot_ref = refs[3 + 3 * n_seg]
    n_q = q_ref.shape[1] // tq

    for vt_ref, va_ref in zip(vt_refs, va_refs):
        tk = vt_ref.shape[2]
        ones_row = jnp.where(lax.broadcasted_iota(jnp.int32, (VAUG_ROWS - V_HEAD, tk), 0) == 0, 1.0, 0.0).astype(BF16)
        for hh in range(2):
            va_ref[hh, :V_HEAD, :] = vt_ref[0, V_HEAD * hh:V_HEAD * (hh + 1), :]
            va_ref[hh, V_HEAD:, :] = ones_row

    def scores(u):
        j, hh = divmod(u, 2)
        rows = slice(j * tq, (j + 1) * tq)
        sl = slice(HEAD_PAD * hh, HEAD_PAD * (hh + 1))
        qb = q_ref[0, rows, sl]
        ss, m = [], None
        for k in k_refs:
            s = lax.dot_general(k[0, :, sl], qb, NT_DIMS, preferred_element_type=F32)
            ss.append(s)
            mm = jnp.max(s, axis=0, keepdims=True)
            m = mm if m is None else jnp.maximum(m, mm)
        return ss, m

    def values(u, sm):
        j, hh = divmod(u, 2)
        rows = slice(j * tq, (j + 1) * tq)
        ss, m = sm
        oa = None
        for s, va in zip(ss, va_refs):
            t = _dot(va[hh], jnp.exp2(s - m).astype(BF16))
            oa = t if oa is None else oa + t
        ot_ref[V_HEAD * hh:V_HEAD * (hh + 1), :] = oa[:V_HEAD] * (1.0 / oa[V_HEAD:V_HEAD + 1])
        if hh == 1:
            o_ref[0, rows, :] = (ot_ref[...].T * sg_ref[0, rows, :].astype(F32)).astype(BF16)

    n_units = 2 * n_q
    prev = scores(0)
    for u in range(1, n_units):
        cur = scores(u)
        values(u - 1, prev)
        prev = cur
    values(n_units - 1, prev)


def _attention(q, ks, vts, sg, tq):
    B, Tq, _ = q.shape
    n_seg = len(ks)
    pair = lambda t, width: pl.BlockSpec((1, t, width), lambda b, p: (b, 0, p))
    return pl.pallas_call(
        functools.partial(_attn_kernel, n_seg=n_seg, tq=tq),
        grid=(B, MLA_HEADS // 2),
        in_specs=([pair(Tq, 2 * HEAD_PAD)]
                  + [pair(k.shape[1], 2 * HEAD_PAD) for k in ks]
                  + [pl.BlockSpec((1, 2 * V_HEAD, vt.shape[2]), lambda b, p: (b, p, 0)) for vt in vts]
                  + [pair(Tq, 2 * V_HEAD)]),
        out_specs=pair(Tq, 2 * V_HEAD),
        out_shape=jax.ShapeDtypeStruct((B, Tq, MLA_WIDTH), BF16),
        scratch_shapes=([pltpu.VMEM((2, VAUG_ROWS, vt.shape[2]), BF16) for vt in vts]
                        + [pltpu.VMEM((2 * V_HEAD, tq), F32)]),
        compiler_params=_cparams(2),
        name="mla_attention",
    )(q, *ks, *vts, sg)


def _out_kernel(a_ref, w_ref, x_ref, g_ref, npost_ref, o_ref):
    y = _dot(a_ref[0], w_ref[...])
    o_ref[0] = x_ref[0] + g_ref[0] * _rms(y, npost_ref[...])


def _out_project(a, w_out, x, gate, npost, tm):
    B, T, K = a.shape
    D = x.shape[-1]
    return pl.pallas_call(
        _out_kernel,
        grid=(B, T // tm),
        in_specs=[pl.BlockSpec((1, tm, K), lambda b, i: (b, i, 0)),
                  _const_spec((K, D)),
                  pl.BlockSpec((1, tm, D), lambda b, i: (b, i, 0)),
                  pl.BlockSpec((1, 1, D), lambda b, i: (b, 0, 0)),
                  _const_spec((1, D))],
        out_specs=pl.BlockSpec((1, tm, D), lambda b, i: (b, i, 0)),
        out_shape=jax.ShapeDtypeStruct(x.shape, F32),
        compiler_params=_cparams(2),
        name="out_project",
    )(a, w_out, x, gate, npost)


def _ml_inproj_kernel(x_ref, mod_ref, npre_ref, w_ref, xm_ref, sz_ref, og_ref):
    h = _prenorm(x_ref, mod_ref, npre_ref)
    xm_ref[0] = _dot(h, w_ref[:, :ML_INNER]).astype(BF16)
    sz_ref[0] = _silu(_dot(h, w_ref[:, ML_INNER:2 * ML_INNER])).astype(BF16)
    og_ref[0] = jax.nn.sigmoid(_dot(h, w_ref[:, 2 * ML_INNER:])).astype(BF16)


def _ml_inproject(x, mod, npre, w_in, tm):
    B, T, D = x.shape
    row = lambda width: pl.BlockSpec((1, tm, width), lambda b, i: (b, i, 0))
    return pl.pallas_call(
        _ml_inproj_kernel,
        grid=(B, T // tm),
        in_specs=[row(D),
                  pl.BlockSpec((1, 2, D), lambda b, i: (b, 0, 0)),
                  _const_spec((1, D)),
                  _const_spec((D, 3 * ML_INNER))],
        out_specs=[row(ML_INNER)] * 3,
        out_shape=[jax.ShapeDtypeStruct((B, T, ML_INNER), BF16)] * 3,
        compiler_params=_cparams(2),
        name="mlstm_inproject",
    )(x, mod, npre, w_in)


def _ml_feat_kernel(xm_ref, xp_ref, xn_ref, cw_ref, cb_ref, wq_ref, wk_ref, wv_ref, wvt_ref, wif_ref, bif_ref,
                    trif_ref, trib_ref, q_ref, k_ref, vt_ref, xc_ref, gc_ref, gr_ref, xe_ref, *, tm):
    i = pl.program_id(1)
    last = pl.num_programs(1) - 1
    pad = CONV_K // 2
    xm = xm_ref[0]
    half = HALO // 2
    xe_ref[0:half, :] = jnp.where(i > 0, xp_ref[0, half:, :].astype(F32), 0.0)
    xe_ref[half:half + tm, :] = xm.astype(F32)
    xe_ref[half + tm:, :] = jnp.where(i < last, xn_ref[0, :half, :].astype(F32), 0.0)
    acc = cb_ref[...] + cw_ref[0:1, :] * xe_ref[half - pad:half - pad + tm, :]
    for j in range(1, CONV_K):
        acc = acc + cw_ref[j:j + 1, :] * xe_ref[half - pad + j:half - pad + j + tm, :]
    xc = _silu(acc)
    xcb = xc.astype(BF16)
    xc_ref[0] = xcb

    n_tiles = ML_INNER // MXU_DIM
    g = jnp.zeros((tm, LANES), F32) + bif_ref[...]
    for j in range(n_tiles):
        sl = slice(MXU_DIM * j, MXU_DIM * (j + 1))
        qj = _dot(xcb[:, sl], wq_ref[j]).astype(BF16)
        kj = _dot(xcb[:, sl], wk_ref[j])
        vj = _dot(xm[:, sl], wv_ref[j]).astype(BF16)
        q_ref[0, :, sl] = qj
        k_ref[0, :, sl] = (kj * (ML_HEAD_DIM ** -0.5)).astype(BF16)
        vt_ref[0, sl, :] = lax.dot_general(wvt_ref[j], xm[:, sl], NT_DIMS,
                                           preferred_element_type=F32).astype(BF16)
        g = g + _dot(qj, wif_ref[sl, :])
        g = g + _dot(kj.astype(BF16), wif_ref[ML_INNER + MXU_DIM * j:ML_INNER + MXU_DIM * (j + 1), :])
        g = g + _dot(vj, wif_ref[2 * ML_INNER + MXU_DIM * j:2 * ML_INNER + MXU_DIM * (j + 1), :])

    lane = lax.broadcasted_iota(jnp.int32, g.shape, 1)
    sub = lane % GATE_LANES
    is_f_fwd = sub == 1
    is_f_bwd = sub == 3
    logsig = jnp.minimum(g, 0.0) - jnp.log(1.0 + jnp.exp(-jnp.abs(g)))
    gs = jnp.where(is_f_fwd | is_f_bwd, logsig, g)
    g1, g2, g3 = _split3(gs)
    cum_f = _dot(trif_ref[...], g1) + _dot(trif_ref[...], g2) + _dot(trif_ref[...], g3)
    cum_b = _dot(trib_ref[...], g1) + _dot(trib_ref[...], g2) + _dot(trib_ref[...], g3)
    out = jnp.where(is_f_fwd, cum_f, jnp.where(is_f_bwd, cum_b, gs))
    gr_ref[0] = out.T[:ML_HEADS * GATE_LANES, :]
    diff = out - pltpu.roll(out, LANES - 1, 1)
    for h in range(ML_HEADS):
        gc_ref[0, h] = diff[:, GATE_LANES * h:GATE_LANES * (h + 1)]


def _ml_features(xm, w, tm):
    B, T, C = xm.shape
    nh = tm // HALO
    n_halo = T // HALO
    row = pl.BlockSpec((1, tm, C), lambda b, i: (b, i, 0))
    n_tiles = C // MXU_DIM
    tile_w = _const_spec((n_tiles, MXU_DIM, MXU_DIM))
    return pl.pallas_call(
        functools.partial(_ml_feat_kernel, tm=tm),
        grid=(B, T // tm),
        in_specs=[row,
                  pl.BlockSpec((1, HALO, C), lambda b, i: (b, jnp.maximum(i * nh - 1, 0), 0)),
                  pl.BlockSpec((1, HALO, C), lambda b, i: (b, jnp.minimum((i + 1) * nh, n_halo - 1), 0)),
                  _const_spec((CONV_K, C)),
                  _const_spec((1, C)),
                  tile_w, tile_w, tile_w, tile_w,
                  _const_spec((3 * C, LANES)),
                  _const_spec((1, LANES)),
                  _const_spec((tm, tm)),
                  _const_spec((tm, tm))],
        out_specs=[row, row,
                   pl.BlockSpec((1, C, tm), lambda b, i: (b, 0, i)),
                   row,
                   pl.BlockSpec((1, ML_HEADS, tm, GATE_LANES), lambda b, i: (b, 0, i, 0)),
                   pl.BlockSpec((1, ML_HEADS * GATE_LANES, tm), lambda b, i: (b, 0, i))],
        out_shape=[jax.ShapeDtypeStruct((B, T, C), BF16),
                   jax.ShapeDtypeStruct((B, T, C), BF16),
                   jax.ShapeDtypeStruct((B, C, T), BF16),
                   jax.ShapeDtypeStruct((B, T, C), BF16),
                   jax.ShapeDtypeStruct((B, ML_HEADS, T, GATE_LANES), F32),
                   jax.ShapeDtypeStruct((B, ML_HEADS * GATE_LANES, T), F32)],
        scratch_shapes=[pltpu.VMEM((tm + HALO, C), F32)],
        compiler_params=_cparams(2),
        name="mlstm_features",
    )(xm, xm, xm, w["conv_w"], w["conv_b"], w["w_q"], w["w_k"], w["w_v"], w["w_vt"], w["w_if"], w["b_if"],
      w["tri_f"], w["tri_b"])


STATE_ROWS = ML_HEAD_DIM + 16


def _scan_chunk(q_ref, k_ref, vt_ref, gc_ref, gr_ref, ht_ref, cn_ref, m, r0, hh, d, with_out):
    L = SCAN_CHUNK
    dh = ML_HEAD_DIM
    rows = pl.ds(r0, L)
    hl = slice(dh * hh, dh * (hh + 1))
    kk = k_ref[0, rows, hl]
    vt = vt_ref[0, hl, rows]
    g0 = GATE_LANES * hh + 2 * d
    ig_r = gr_ref[0, g0:g0 + 1, rows]
    b_r = gr_ref[0, g0 + 1:g0 + 2, rows]
    b_end = b_r[:, L - 1:L] if d == 0 else b_r[:, 0:1]
    w_r = b_end - b_r + ig_r
    m_new = jnp.maximum(b_end + m, jnp.max(w_r, axis=-1, keepdims=True))
    a_r = jnp.exp(w_r - m_new)
    decay = jnp.exp(b_end + m - m_new)
    cn = cn_ref[hh, d]
    if with_out:
        qq = q_ref[0, rows, hl]
        u_c = gc_ref[0, hh, rows, 2 * d:2 * d + 1]
        si = lax.broadcasted_iota(jnp.int32, (L, L), 0)
        ti = lax.broadcasted_iota(jnp.int32, (L, L), 1)
        mask = (si <= ti) if d == 0 else (si >= ti)
        log_d = jnp.where(mask, u_c + b_r, -jnp.inf)
        m_t = jnp.maximum(b_r + m, jnp.max(log_d, axis=0, keepdims=True))
        inter = jnp.exp(b_r + m - m_t)
        s = lax.dot_general(kk, qq, NT_DIMS, preferred_element_type=F32) * jnp.exp(log_d - m_t)
        cq = lax.dot_general(cn.astype(BF16), qq, NT_DIMS, preferred_element_type=F32)
        num = _dot(vt, s.astype(BF16)) + inter * cq[:dh]
        den = jnp.sum(s, axis=0, keepdims=True) + inter * cq[dh:dh + 1]
        ht_ref[hl, rows] += num * (1.0 / jnp.maximum(jnp.abs(den), jnp.exp(-m_t)))
    sub = lax.broadcasted_iota(jnp.int32, (STATE_ROWS - dh, L), 0)
    va = jnp.concatenate([(vt.astype(F32) * a_r).astype(BF16),
                          jnp.where(sub == 0, a_r, 0.0).astype(BF16)], axis=0)
    cn_ref[hh, d] = decay * cn + _dot(va, kk)
    return m_new


def _scan_finish(ht_ref, og_ref, xc_ref, sz_ref, hn_ref, skip_ref, y_ref):
    L = SCAN_CHUNK
    dh = ML_HEAD_DIM

    def body(i, carry):
        rows = pl.ds(pl.multiple_of(i * L, L), L)
        for hh in range(SCAN_HEADS):
            hl = slice(dh * hh, dh * (hh + 1))
            h = og_ref[0, rows, hl].astype(F32) * ht_ref[hl, rows].T
            hn = h * lax.rsqrt(jnp.mean(h * h, axis=-1, keepdims=True) + EPS)
            y = ((hn * hn_ref[:, hl] + skip_ref[:, hl] * xc_ref[0, rows, hl].astype(F32))
                 * sz_ref[0, rows, hl].astype(F32))
            y_ref[0, rows, hl] = y.astype(BF16)
        return carry

    lax.fori_loop(0, ht_ref.shape[1] // L, body, 0)


def _ml_scan_kernel(*refs, ctx_out):
    (ql, kl, vtl, gcl, grl, ogl, xcl, szl, qc, kc, vtc, gcc, grc) = refs[:13]
    pos = 13
    if ctx_out:
        ogc, xcc, szc = refs[pos:pos + 3]
        pos += 3
    hn_ref, skip_ref = refs[pos:pos + 2]
    pos += 2
    yl_ref = refs[pos]
    pos += 1
    if ctx_out:
        yc_ref = refs[pos]
        pos += 1
    hl_ref, hc_ref, cn_ref = refs[pos:pos + 3]
    L = SCAN_CHUNK
    n_l = ql.shape[1] // L
    n_c = kc.shape[1] // L
    chains = [(hh, d) for hh in range(SCAN_HEADS) for d in range(2)]

    cn_ref[...] = jnp.zeros_like(cn_ref)
    hl_ref[...] = jnp.zeros_like(hl_ref)
    if ctx_out:
        hc_ref[...] = jnp.zeros_like(hc_ref)
    ms = [jnp.zeros((1, 1), F32) for _ in chains]
    for c in range(n_c):
        ms = [_scan_chunk(qc, kc, vtc, gcc, grc, hc_ref, cn_ref, m, (c if d == 0 else n_c - 1 - c) * L, hh, d,
                          ctx_out) for (hh, d), m in zip(chains, ms)]

    def body(i, carry):
        return tuple(
            _scan_chunk(ql, kl, vtl, gcl, grl, hl_ref, cn_ref, m,
                        pl.multiple_of((i if d == 0 else n_l - 1 - i) * L, L), hh, d, True)
            for (hh, d), m in zip(chains, carry))

    lax.fori_loop(0, n_l, body, tuple(ms))

    _scan_finish(hl_ref, ogl, xcl, szl, hn_ref, skip_ref, yl_ref)
    if ctx_out:
        _scan_finish(hc_ref, ogc, xcc, szc, hn_ref, skip_ref, yc_ref)


def _ml_scan(lat, ctx, head_norm, skip, ctx_out):
    B, T, C = lat["q"].shape
    Tc = ctx["q"].shape[1]
    wd = ML_HEAD_DIM * SCAN_HEADS

    def stream_specs(t, names):
        specs = []
        for nm in names:
            if nm == "gc":
                specs.append(pl.BlockSpec((1, SCAN_HEADS, t, GATE_LANES), lambda b, h: (b, h, 0, 0)))
            elif nm == "gr":
                specs.append(pl.BlockSpec((1, SCAN_HEADS * GATE_LANES, t), lambda b, h: (b, h, 0)))
            elif nm == "vt":
                specs.append(pl.BlockSpec((1, wd, t), lambda b, h: (b, h, 0)))
            else:
                specs.append(pl.BlockSpec((1, t, wd), lambda b, h: (b, 0, h)))
        return specs

    lat_names = ["q", "k", "vt", "gc", "gr", "og", "xc", "sz"]
    ctx_names = ["q", "k", "vt", "gc", "gr"] + (["og", "xc", "sz"] if ctx_out else [])
    head_vec = pl.BlockSpec((1, wd), lambda b, h: (0, h))
    out_specs = [pl.BlockSpec((1, T, wd), lambda b, h: (b, 0, h))]
    out_shape = [jax.ShapeDtypeStruct((B, T, C), BF16)]
    if ctx_out:
        out_specs.append(pl.BlockSpec((1, Tc, wd), lambda b, h: (b, 0, h)))
        out_shape.append(jax.ShapeDtypeStruct((B, Tc, C), BF16))
    outs = pl.pallas_call(
        functools.partial(_ml_scan_kernel, ctx_out=ctx_out),
        grid=(B, ML_HEADS // SCAN_HEADS),
        in_specs=stream_specs(T, lat_names) + stream_specs(Tc, ctx_names) + [head_vec, head_vec],
        out_specs=out_specs,
        out_shape=out_shape,
        scratch_shapes=[pltpu.VMEM((wd, T), F32), pltpu.VMEM((wd, Tc), F32),
                        pltpu.VMEM((SCAN_HEADS, 2, STATE_ROWS, ML_HEAD_DIM), F32)],
        compiler_params=_cparams(2),
        name="mlstm_scan",
    )(*[lat[nm] for nm in lat_names], *[ctx[nm] for nm in ctx_names], head_norm, skip)
    return (outs[0], outs[1]) if ctx_out else (outs[0], None)


def _swap_pairs(a):
    g = a.reshape(a.shape[:-1] + (2, 2, QK_ROPE // 4))
    return jnp.flip(g, axis=-2).reshape(a.shape)


def _prep_mla_weights(w_in, q_norm, kv_norm, w_uq, w_ukv, w_out):
    kr0 = Q_LORA + KV_LORA
    kr = w_in[:, kr0:kr0 + QK_ROPE]
    krs = _swap_pairs(kr)
    w_in_p = jnp.concatenate([w_in[:, :kr0], kr, krs, kr, krs, w_in[:, kr0 + QK_ROPE:]], axis=1)
    uq = w_uq.reshape(Q_LORA, MLA_HEADS, QK_HEAD)
    rope = uq[..., QK_NOPE:]
    uq_p = jnp.concatenate([uq[..., :QK_NOPE], rope, _swap_pairs(rope)], axis=-1).reshape(Q_LORA, MLA_QW)
    ukv = w_ukv.reshape(KV_LORA, MLA_HEADS, QK_NOPE + V_HEAD)
    uk_p = jnp.concatenate([ukv[..., :QK_NOPE], jnp.zeros((KV_LORA, MLA_HEADS, HEAD_PAD - QK_NOPE), w_ukv.dtype)],
                           axis=-1).reshape(KV_LORA, MLA_QW)
    uv = ukv[..., QK_NOPE:].reshape(KV_LORA, MLA_WIDTH)
    return {"w_in": w_in_p.astype(BF16), "q_norm": q_norm.reshape(1, -1), "kv_norm": kv_norm.reshape(1, -1),
            "w_uq": uq_p.astype(BF16), "w_uk": uk_p.astype(BF16), "w_uvt": uv.T.astype(BF16),
            "w_out": w_out.astype(BF16)}


def _rope_tables(n_tokens, rotate):
    if rotate:
        rows = n_tokens // GRID_W
        row = jnp.repeat(jnp.arange(rows, dtype=jnp.int32), GRID_W).astype(F32)
        col = jnp.tile(jnp.arange(GRID_W, dtype=jnp.int32), rows).astype(F32)
        qd = QK_ROPE // 4
        inv = ROPE_THETA ** (-jnp.arange(qd, dtype=F32) / qd)
        ang = jnp.stack([row[:, None] * inv, col[:, None] * inv], axis=1)
        cos, sin = jnp.cos(ang), jnp.sin(ang)
    else:
        cos = jnp.ones((n_tokens, 2, QK_ROPE // 4), F32)
        sin = jnp.zeros((n_tokens, 2, QK_ROPE // 4), F32)
    cf = jnp.stack([cos, cos], axis=2).reshape(n_tokens, QK_ROPE)
    sf = jnp.stack([-sin, sin], axis=2).reshape(n_tokens, QK_ROPE)
    tabq = (ATTN_SCALE * LOG2_E) * jnp.concatenate([jnp.ones((n_tokens, QK_NOPE), F32), cf, sf], axis=1)
    tabk = jnp.concatenate([cf, sf, cf, sf], axis=1)
    return tabq, tabk


def _dense_blockdiag(w):
    per = MXU_DIM // QKV_BLOCK
    wt = w.reshape(-1, per, QKV_BLOCK, QKV_BLOCK)
    eye = jnp.eye(per, dtype=w.dtype)
    dense = jnp.einsum('jncd,nm->jncmd', wt, eye)
    return dense.reshape(-1, MXU_DIM, MXU_DIM).astype(BF16)


def _prep_ml_weights(w_in, conv_w, conv_b, w_q, w_k, w_v, w_if, b_if, head_norm, skip, w_out, tm):
    H = ML_HEADS
    cols = jnp.stack([w_if[0, :, :H], w_if[0, :, H:], w_if[1, :, :H], w_if[1, :, H:]], axis=-1)
    cols = jnp.concatenate([cols, jnp.zeros(cols.shape[:2] + (GATE_LANES - 4,), cols.dtype)], axis=-1)
    w_if_p = jnp.concatenate([cols.reshape(cols.shape[0], H * GATE_LANES),
                              jnp.zeros((cols.shape[0], LANES - H * GATE_LANES), cols.dtype)], axis=1)
    bcols = jnp.stack([b_if[0, :H], b_if[0, H:], b_if[1, :H], b_if[1, H:]], axis=-1)
    bcols = jnp.concatenate([bcols, jnp.zeros((H, GATE_LANES - 4), bcols.dtype)], axis=-1).reshape(1, -1)
    b_if_p = jnp.concatenate([bcols, jnp.zeros((1, LANES - H * GATE_LANES), bcols.dtype)], axis=1)
    t = jnp.arange(tm)
    same = (t[:, None] // SCAN_CHUNK) == (t[None, :] // SCAN_CHUNK)
    tri_f = (same & (t[None, :] <= t[:, None])).astype(BF16)
    tri_b = (same & (t[None, :] >= t[:, None])).astype(BF16)
    return {"w_in": w_in.astype(BF16), "conv_w": conv_w, "conv_b": conv_b.reshape(1, -1),
            "w_q": _dense_blockdiag(w_q), "w_k": _dense_blockdiag(w_k), "w_v": _dense_blockdiag(w_v),
            "w_vt": jnp.swapaxes(_dense_blockdiag(w_v), 1, 2),
            "w_if": w_if_p.astype(BF16), "b_if": b_if_p, "tri_f": tri_f, "tri_b": tri_b,
            "head_norm": head_norm.reshape(1, -1), "skip": skip.reshape(1, -1), "w_out": w_out.astype(BF16)}


def _row_tile(t, want):
    tm = min(t, want)
    assert t % tm == 0, (t, tm)
    return tm


def kernel(x, c, ctx, c_ctx, ada_w, ada_b, norm_pre, norm_post, mla_w_in, mla_q_norm, mla_kv_norm, mla_w_uq,
           mla_w_ukv, mla_w_out, ml_w_in, ml_conv_w, ml_conv_b, ml_w_q, ml_w_k, ml_w_v, ml_w_if, ml_b_if,
           ml_head_norm, ml_skip, ml_w_out):
    B, T, D = x.shape
    Tc = ctx.shape[1]
    assert D == D_MODEL and T % GRID_W == 0 and T % SCAN_CHUNK == 0 and Tc % SCAN_CHUNK == 0
    tm_l = _row_tile(T, 512)
    tm_c = _row_tile(Tc, 256)
    tf_l = _row_tile(T, 256)
    tf_c = _row_tile(Tc, 256)
    tq_l = _row_tile(T, 512)
    tq_c = _row_tile(Tc, 256)

    n_rows = -(-(B + 1) // 8) * 8
    cc = jnp.concatenate([c, c_ctx[None, :], jnp.zeros((n_rows - B - 1, D), F32)], axis=0)
    mods = _modulation(cc, ada_w, ada_b)

    tabq_l, tabk_l = _rope_tables(T, True)
    tabq_c, tabk_c = _rope_tables(Tc, False)

    for i in range(DEPTH):
        need_ctx_out = i < DEPTH - 1
        j = i // 2
        m_l = mods[i, :B]
        m_c = jnp.broadcast_to(mods[i, B:B + 1], (B, 3 * D))
        mod_l = jnp.stack([m_l[:, :D], m_l[:, D:2 * D]], axis=1)
        mod_c = jnp.stack([m_c[:, :D], m_c[:, D:2 * D]], axis=1)
        gate_l = m_l[:, None, 2 * D:]
        gate_c = m_c[:, None, 2 * D:]
        npre = norm_pre[i].reshape(1, D)
        npost = norm_post[i].reshape(1, D)
        if i % 2 == 0:
            w = _prep_mla_weights(mla_w_in[j], mla_q_norm[j], mla_kv_norm[j], mla_w_uq[j], mla_w_ukv[j],
                                  mla_w_out[j])
            ql, kl, vtl, sgl = _mla_project(x, mod_l, npre, w, tabq_l, tabk_l, tm_l)
            qc, kc, vtc, sgc = _mla_project(ctx, mod_c, npre, w, tabq_c, tabk_c, tm_c)
            al = _attention(ql, [kc, kl], [vtc, vtl], sgl, tq_l)
            ac = _attention(qc, [kc], [vtc], sgc, tq_c) if need_ctx_out else None
            w_out = w["w_out"]
        else:
            w = _prep_ml_weights(ml_w_in[j], ml_conv_w[j], ml_conv_b[j], ml_w_q[j], ml_w_k[j], ml_w_v[j],
                                 ml_w_if[j], ml_b_if[j], ml_head_norm[j], ml_skip[j], ml_w_out[j], tf_l)
            assert tf_l == tf_c
            streams = []
            for xs, mod, tm, tf in ((x, mod_l, tm_l, tf_l), (ctx, mod_c, tm_c, tf_c)):
                xm, sz, og = _ml_inproject(xs, mod, npre, w["w_in"], tm)
                q, k, vt, xc, gc, gr = _ml_features(xm, w, tf)
                streams.append({"q": q, "k": k, "vt": vt, "gc": gc, "gr": gr, "og": og, "xc": xc, "sz": sz})
            al, ac = _ml_scan(streams[0], streams[1], w["head_norm"], w["skip"], need_ctx_out)
            w_out = w["w_out"]
        x = _out_project(al, w_out, x, gate_l, npost, tm_l)
        if need_ctx_out:
            ctx = _out_project(ac, w_out, ctx, gate_c, npost, tm_c)
    return x
```

```python
import functools

import jax
import jax.numpy as jnp
from jax import lax
from jax.experimental import pallas as pl
from jax.experimental.pallas import tpu as pltpu

F32 = jnp.float32
BF16 = jnp.bfloat16

D_MODEL = 1024
DEPTH = 4
GRID_W = 64
EPS = 1e-6
ROPE_THETA = 10000.0
MLA_HEADS = 16
QK_NOPE = 64
QK_ROPE = 32
V_HEAD = 64
Q_LORA = 256
KV_LORA = 256
QK_HEAD = QK_NOPE + QK_ROPE
MLA_WIDTH = MLA_HEADS * V_HEAD
ATTN_SCALE = QK_HEAD ** -0.5
LOG2_E = 1.4426950408889634
ML_INNER = 2 * D_MODEL
ML_HEADS = 8
ML_HEAD_DIM = ML_INNER // ML_HEADS
QKV_BLOCK = 4
CONV_K = 5

LANES = 128
MXU_DIM = 256
VMEM_LIMIT = 56 * 1024 * 1024

HEAD_PAD = LANES
MLA_QW = MLA_HEADS * HEAD_PAD
SCAN_CHUNK = MXU_DIM
HALO = 16
GATE_LANES = 8
SCAN_HEADS = 2

NT_DIMS = (((1,), (1,)), ((), ()))
TN_DIMS = (((0,), (0,)), ((), ()))


def _cparams(n_axes):
    return pltpu.CompilerParams(dimension_semantics=("parallel",) * n_axes,
                                vmem_limit_bytes=VMEM_LIMIT)


def _const_spec(shape):
    nd = len(shape)
    return pl.BlockSpec(shape, lambda *_: (0,) * nd)


def _resident_spec(shape):
    nd = len(shape)
    return pl.BlockSpec(shape, lambda *_: (0,) * nd, pipeline_mode=pl.Buffered(1))


def _rms(x, g):
    r = lax.rsqrt(jnp.mean(x * x, axis=-1, keepdims=True) + EPS)
    return (x * r) * g


def _silu(x):
    return x * jax.nn.sigmoid(x)


def _split3(x):
    x1 = x.astype(BF16)
    r1 = x - x1.astype(F32)
    x2 = r1.astype(BF16)
    x3 = (r1 - x2.astype(F32)).astype(BF16)
    return x1, x2, x3


def _dot(a, b):
    return jnp.dot(a, b, preferred_element_type=F32)


def _mod_kernel(c_ref, w_ref, b_ref, o_ref):
    a1, a2, a3 = _split3(_silu(c_ref[...]))
    w = w_ref[0]
    w1 = w.astype(BF16)
    w2 = (w - w1.astype(F32)).astype(BF16)
    acc = _dot(a1, w1) + (_dot(a1, w2) + _dot(a2, w1)) + (_dot(a2, w2) + _dot(a3, w1))
    o_ref[0] = acc + b_ref[0]


def _modulation(cc, ada_w, ada_b):
    rows = cc.shape[0]
    nblk = ada_w.shape[2] // D_MODEL
    return pl.pallas_call(
        _mod_kernel,
        grid=(DEPTH, nblk),
        in_specs=[pl.BlockSpec((rows, D_MODEL), lambda i, j: (0, 0)),
                  pl.BlockSpec((1, D_MODEL, D_MODEL), lambda i, j: (i, 0, j)),
                  pl.BlockSpec((1, 1, D_MODEL), lambda i, j: (i, 0, j))],
        out_specs=pl.BlockSpec((1, rows, D_MODEL), lambda i, j: (i, 0, j)),
        out_shape=jax.ShapeDtypeStruct((DEPTH, rows, ada_w.shape[2]), F32),
        compiler_params=_cparams(2),
        name="modulation",
    )(cc, ada_w, ada_b.reshape(DEPTH, 1, -1))


def _prenorm(x_ref, mod_ref, npre_ref):
    h = _rms(x_ref[0], npre_ref[...]) * (1.0 + mod_ref[0, 1:2, :]) + mod_ref[0, 0:1, :]
    return h.astype(BF16)


def _mla_proj_kernel(x_ref, mod_ref, npre_ref, win_ref, qn_ref, kvn_ref, wuq_ref, wuk_ref, wuvt_ref,
                     tabq_ref, tabk_ref, q_ref, k_ref, vt_ref, sg_ref):
    h = _prenorm(x_ref, mod_ref, npre_ref)
    u = _dot(h, win_ref[...])
    cq = u[:, :Q_LORA]
    ckv = u[:, Q_LORA:Q_LORA + KV_LORA]
    krp = u[:, Q_LORA + KV_LORA:Q_LORA + KV_LORA + LANES]
    gate = u[:, Q_LORA + KV_LORA + LANES:]
    ckvn = _rms(ckv, kvn_ref[...]).astype(BF16)
    q = _dot(_rms(cq, qn_ref[...]).astype(BF16), wuq_ref[...])
    kn = _dot(ckvn, wuk_ref[...])
    vt_ref[0] = lax.dot_general(wuvt_ref[...], ckvn, NT_DIMS, preferred_element_type=F32).astype(BF16)
    t = krp * tabk_ref[...]
    r = t + pltpu.roll(t, QK_ROPE, 1)
    lane = lax.broadcasted_iota(jnp.int32, r.shape, 1)
    rk = jnp.where(lane >= QK_NOPE, r, 0.0)
    tabq = tabq_ref[...]
    for hh in range(MLA_HEADS):
        sl = slice(HEAD_PAD * hh, HEAD_PAD * (hh + 1))
        q_ref[0, :, sl] = (q[:, sl] * tabq).astype(BF16)
        k_ref[0, :, sl] = (kn[:, sl] + rk).astype(BF16)
    sg_ref[0] = _silu(gate).astype(BF16)


def _mla_project(x, mod, npre, w, tabq, tabk, tm):
    B, T, D = x.shape
    nin = w["w_in"].shape[1]
    row = lambda width: pl.BlockSpec((1, tm, width), lambda b, i: (b, i, 0))
    return pl.pallas_call(
        _mla_proj_kernel,
        grid=(B, T // tm),
        in_specs=[row(D),
                  pl.BlockSpec((1, 2, D), lambda b, i: (b, 0, 0)),
                  _const_spec((1, D)),
                  _const_spec((D, nin)),
                  _const_spec((1, Q_LORA)),
                  _const_spec((1, KV_LORA)),
                  _const_spec((Q_LORA, MLA_QW)),
                  _const_spec((KV_LORA, MLA_QW)),
                  _const_spec((MLA_WIDTH, KV_LORA)),
                  pl.BlockSpec((tm, LANES), lambda b, i: (i, 0)),
                  pl.BlockSpec((tm, LANES), lambda b, i: (i, 0))],
        out_specs=[row(MLA_QW), row(MLA_QW),
                   pl.BlockSpec((1, MLA_WIDTH, tm), lambda b, i: (b, 0, i)),
                   row(MLA_WIDTH)],
        out_shape=[jax.ShapeDtypeStruct((B, T, MLA_QW), BF16),
                   jax.ShapeDtypeStruct((B, T, MLA_QW), BF16),
                   jax.ShapeDtypeStruct((B, MLA_WIDTH, T), BF16),
                   jax.ShapeDtypeStruct((B, T, MLA_WIDTH), BF16)],
        compiler_params=_cparams(2),
        name="mla_project",
    )(x, mod, npre, w["w_in"], w["q_norm"], w["kv_norm"], w["w_uq"], w["w_uk"], w["w_uvt"], tabq, tabk)


VAUG_ROWS = V_HEAD + 16


def _attn_kernel(*refs, n_seg, tq):
    q_ref = refs[0]
    k_refs = refs[1:1 + n_seg]
    vt_refs = refs[1 + n_seg:1 + 2 * n_seg]
    sg_ref = refs[1 + 2 * n_seg]
    o_ref = refs[2 + 2 * n_seg]
    va_refs = refs[3 + 2 * n_seg:3 + 3 * n_seg]
    ot_ref = refs[3 + 3 * n_seg]
    n_q = q_ref.shape[1] // tq

    for vt_ref, va_ref in zip(vt_refs, va_refs):
        tk = vt_ref.shape[2]
        ones_row = jnp.where(lax.broadcasted_iota(jnp.int32, (VAUG_ROWS - V_HEAD, tk), 0) == 0, 1.0, 0.0).astype(BF16)
        for hh in range(2):
            va_ref[hh, :V_HEAD, :] = vt_ref[0, V_HEAD * hh:V_HEAD * (hh + 1), :]
            va_ref[hh, V_HEAD:, :] = ones_row

    def scores(u):
        j, hh = divmod(u, 2)
        rows = slice(j * tq, (j + 1) * tq)
        sl = slice(HEAD_PAD * hh, HEAD_PAD * (hh + 1))
        qb = q_ref[0, rows, sl]
        ss, m = [], None
        for k in k_refs:
            s = lax.dot_general(k[0, :, sl], qb, NT_DIMS, preferred_element_type=F32)
            ss.append(s)
            mm = jnp.max(s, axis=0, keepdims=True)
            m = mm if m is None else jnp.maximum(m, mm)
        return ss, m

    def values(u, sm):
        j, hh = divmod(u, 2)
        rows = slice(j * tq, (j + 1) * tq)
        ss, m = sm
        oa = None
        for s, va in zip(ss, va_refs):
            t = _dot(va[hh], jnp.exp2(s - m).astype(BF16))
            oa = t if oa is None else oa + t
        ot_ref[V_HEAD * hh:V_HEAD * (hh + 1), :] = oa[:V_HEAD] * (1.0 / oa[V_HEAD:V_HEAD + 1])
        if hh == 1:
            o_ref[0, rows, :] = (ot_ref[...].T * sg_ref[0, rows, :].astype(F32)).astype(BF16)

    n_units = 2 * n_q
    prev = scores(0)
    for u in range(1, n_units):
        cur = scores(u)
        values(u - 1, prev)
        prev = cur
    values(n_units - 1, prev)


def _attention(q, ks, vts, sg, tq):
    B, Tq, _ = q.shape
    n_seg = len(ks)
    pair = lambda t, width: pl.BlockSpec((1, t, width), lambda b, p: (b, 0, p))
    return pl.pallas_call(
        functools.partial(_attn_kernel, n_seg=n_seg, tq=tq),
        grid=(B, MLA_HEADS // 2),
        in_specs=([pair(Tq, 2 * HEAD_PAD)]
                  + [pair(k.shape[1], 2 * HEAD_PAD) for k in ks]
                  + [pl.BlockSpec((1, 2 * V_HEAD, vt.shape[2]), lambda b, p: (b, p, 0)) for vt in vts]
                  + [pair(Tq, 2 * V_HEAD)]),
        out_specs=pair(Tq, 2 * V_HEAD),
        out_shape=jax.ShapeDtypeStruct((B, Tq, MLA_WIDTH), BF16),
        scratch_shapes=([pltpu.VMEM((2, VAUG_ROWS, vt.shape[2]), BF16) for vt in vts]
                        + [pltpu.VMEM((2 * V_HEAD, tq), F32)]),
        compiler_params=_cparams(2),
        name="mla_attention",
    )(q, *ks, *vts, sg)


def _out_kernel(a_ref, w_ref, x_ref, g_ref, npost_ref, o_ref):
    y = _dot(a_ref[0], w_ref[...])
    o_ref[0] = x_ref[0] + g_ref[0] * _rms(y, npost_ref[...])


def _out_project(a, w_out, x, gate, npost, tm):
    B, T, K = a.shape
    D = x.shape[-1]
    return pl.pallas_call(
        _out_kernel,
        grid=(B, T // tm),
        in_specs=[pl.BlockSpec((1, tm, K), lambda b, i: (b, i, 0)),
                  _const_spec((K, D)),
                  pl.BlockSpec((1, tm, D), lambda b, i: (b, i, 0)),
                  pl.BlockSpec((1, 1, D), lambda b, i: (b, 0, 0)),
                  _const_spec((1, D))],
        out_specs=pl.BlockSpec((1, tm, D), lambda b, i: (b, i, 0)),
        out_shape=jax.ShapeDtypeStruct(x.shape, F32),
        compiler_params=_cparams(2),
        name="out_project",
    )(a, w_out, x, gate, npost)


def _ml_front_kernel(x_ref, xp_ref, xn_ref, mod_ref, npre_ref, win_ref, cw_ref, cb_ref, wq_ref, wk_ref, wv_ref,
                     wvt_ref, wif_ref, bif_ref, trif_ref, trib_ref,
                     q_ref, k_ref, vt_ref, xc_ref, sz_ref, og_ref, gc_ref, gr_ref, xe_ref, *, tm):
    i = pl.program_id(1)
    last = pl.num_programs(1) - 1
    pad = CONV_K // 2
    half = HALO // 2

    def prenorm(xv):
        return _rms(xv, npre_ref[...]) * (1.0 + mod_ref[0, 1:2, :]) + mod_ref[0, 0:1, :]

    hx = jnp.concatenate([prenorm(xp_ref[0]), prenorm(x_ref[0]), prenorm(xn_ref[0])], axis=0).astype(BF16)
    xme = _dot(hx, win_ref[:, :ML_INNER])
    xe_ref[0:half, :] = jnp.where(i > 0, xme[0:half], 0.0)
    xe_ref[half:half + tm, :] = xme[half:half + tm]
    xe_ref[half + tm:, :] = jnp.where(i < last, xme[half + tm:], 0.0)
    h = hx[half:half + tm]
    xm = xme[half:half + tm].astype(BF16)
    sz_ref[0] = _silu(_dot(h, win_ref[:, ML_INNER:2 * ML_INNER])).astype(BF16)
    og_ref[0] = jax.nn.sigmoid(_dot(h, win_ref[:, 2 * ML_INNER:])).astype(BF16)

    acc = cb_ref[...] + cw_ref[0:1, :] * xe_ref[half - pad:half - pad + tm, :]
    for j in range(1, CONV_K):
        acc = acc + cw_ref[j:j + 1, :] * xe_ref[half - pad + j:half - pad + j + tm, :]
    xc = _silu(acc)
    xcb = xc.astype(BF16)
    xc_ref[0] = xcb

    n_tiles = ML_INNER // MXU_DIM
    g = jnp.zeros((tm, LANES), F32) + bif_ref[...]
    for j in range(n_tiles):
        sl = slice(MXU_DIM * j, MXU_DIM * (j + 1))
        qj = _dot(xcb[:, sl], wq_ref[j]).astype(BF16)
        kj = _dot(xcb[:, sl], wk_ref[j])
        vj = _dot(xm[:, sl], wv_ref[j]).astype(BF16)
        q_ref[0, :, sl] = qj
        k_ref[0, :, sl] = (kj * (ML_HEAD_DIM ** -0.5)).astype(BF16)
        vt_ref[0, sl, :] = lax.dot_general(wvt_ref[j], xm[:, sl], NT_DIMS,
                                           preferred_element_type=F32).astype(BF16)
        g = g + _dot(qj, wif_ref[sl, :])
        g = g + _dot(kj.astype(BF16), wif_ref[ML_INNER + MXU_DIM * j:ML_INNER + MXU_DIM * (j + 1), :])
        g = g + _dot(vj, wif_ref[2 * ML_INNER + MXU_DIM * j:2 * ML_INNER + MXU_DIM * (j + 1), :])

    lane = lax.broadcasted_iota(jnp.int32, g.shape, 1)
    sub = lane % GATE_LANES
    is_f_fwd = sub == 1
    is_f_bwd = sub == 3
    logsig = jnp.minimum(g, 0.0) - jnp.log(1.0 + jnp.exp(-jnp.abs(g)))
    gs = jnp.where(is_f_fwd | is_f_bwd, logsig, g)
    g1, g2, g3 = _split3(gs)
    cum_f = _dot(trif_ref[...], g1) + _dot(trif_ref[...], g2) + _dot(trif_ref[...], g3)
    cum_b = _dot(trib_ref[...], g1) + _dot(trib_ref[...], g2) + _dot(trib_ref[...], g3)
    out = jnp.where(is_f_fwd, cum_f, jnp.where(is_f_bwd, cum_b, gs))
    gr_ref[0] = out.T[:ML_HEADS * GATE_LANES, :]
    diff = out - pltpu.roll(out, LANES - 1, 1)
    for hd in range(ML_HEADS):
        gc_ref[0, hd] = diff[:, GATE_LANES * hd:GATE_LANES * (hd + 1)]


def _ml_front(x, mod, npre, w, tm):
    B, T, D = x.shape
    C = ML_INNER
    half = HALO // 2
    nh = tm // half
    n_halo = T // half
    row = lambda width: pl.BlockSpec((1, tm, width), lambda b, i: (b, i, 0))
    n_tiles = C // MXU_DIM
    tile_w = _resident_spec((n_tiles, MXU_DIM, MXU_DIM))
    return pl.pallas_call(
        functools.partial(_ml_front_kernel, tm=tm),
        grid=(B, T // tm),
        in_specs=[row(D),
                  pl.BlockSpec((1, half, D), lambda b, i: (b, jnp.maximum(i * nh - 1, 0), 0)),
                  pl.BlockSpec((1, half, D), lambda b, i: (b, jnp.minimum((i + 1) * nh, n_halo - 1), 0)),
                  pl.BlockSpec((1, 2, D), lambda b, i: (b, 0, 0)),
                  _resident_spec((1, D)),
                  _resident_spec((D, 3 * C)),
                  _resident_spec((CONV_K, C)),
                  _resident_spec((1, C)),
                  tile_w, tile_w, tile_w, tile_w,
                  _resident_spec((3 * C, LANES)),
                  _resident_spec((1, LANES)),
                  _resident_spec((tm, tm)),
                  _resident_spec((tm, tm))],
        out_specs=[row(C), row(C),
                   pl.BlockSpec((1, C, tm), lambda b, i: (b, 0, i)),
                   row(C), row(C), row(C),
                   pl.BlockSpec((1, ML_HEADS, tm, GATE_LANES), lambda b, i: (b, 0, i, 0)),
                   pl.BlockSpec((1, ML_HEADS * GATE_LANES, tm), lambda b, i: (b, 0, i))],
        out_shape=[jax.ShapeDtypeStruct((B, T, C), BF16),
                   jax.ShapeDtypeStruct((B, T, C), BF16),
                   jax.ShapeDtypeStruct((B, C, T), BF16),
                   jax.ShapeDtypeStruct((B, T, C), BF16),
                   jax.ShapeDtypeStruct((B, T, C), BF16),
                   jax.ShapeDtypeStruct((B, T, C), BF16),
                   jax.ShapeDtypeStruct((B, ML_HEADS, T, GATE_LANES), F32),
                   jax.ShapeDtypeStruct((B, ML_HEADS * GATE_LANES, T), F32)],
        scratch_shapes=[pltpu.VMEM((tm + HALO, C), F32)],
        compiler_params=_cparams(2),
        name="mlstm_front",
    )(x, x, x, mod, npre, w["w_in"], w["conv_w"], w["conv_b"], w["w_q"], w["w_k"], w["w_v"], w["w_vt"],
      w["w_if"], w["b_if"], w["tri_f"], w["tri_b"])


STATE_ROWS = ML_HEAD_DIM + 16


def _scan_chunk(q_ref, k_ref, vt_ref, gc_ref, gr_ref, ht_ref, cn_ref, m, r0, hh, d, with_out):
    L = SCAN_CHUNK
    dh = ML_HEAD_DIM
    rows = pl.ds(r0, L)
    hl = slice(dh * hh, dh * (hh + 1))
    kk = k_ref[0, rows, hl]
    vt = vt_ref[0, hl, rows]
    g0 = GATE_LANES * hh + 2 * d
    ig_r = gr_ref[0, g0:g0 + 1, rows]
    b_r = gr_ref[0, g0 + 1:g0 + 2, rows]
    b_end = b_r[:, L - 1:L] if d == 0 else b_r[:, 0:1]
    w_r = b_end - b_r + ig_r
    m_new = jnp.maximum(b_end + m, jnp.max(w_r, axis=-1, keepdims=True))
    a_r = jnp.exp(w_r - m_new)
    decay = jnp.exp(b_end + m - m_new)
    cn = cn_ref[hh, d]
    if with_out:
        qq = q_ref[0, rows, hl]
        u_c = gc_ref[0, hh, rows, 2 * d:2 * d + 1]
        si = lax.broadcasted_iota(jnp.int32, (L, L), 0)
        ti = lax.broadcasted_iota(jnp.int32, (L, L), 1)
        mask = (si <= ti) if d == 0 else (si >= ti)
        log_d = jnp.where(mask, u_c + b_r, -jnp.inf)
        m_t = jnp.maximum(b_r + m, jnp.max(log_d, axis=0, keepdims=True))
        inter = jnp.exp(b_r + m - m_t)
        s = lax.dot_general(kk, qq, NT_DIMS, preferred_element_type=F32) * jnp.exp(log_d - m_t)
        cq = lax.dot_general(cn.astype(BF16), qq, NT_DIMS, preferred_element_type=F32)
        num = _dot(vt, s.astype(BF16)) + inter * cq[:dh]
        den = jnp.sum(s, axis=0, keepdims=True) + inter * cq[dh:dh + 1]
        ht_ref[hl, rows] += num * (1.0 / jnp.maximum(jnp.abs(den), jnp.exp(-m_t)))
    sub = lax.broadcasted_iota(jnp.int32, (STATE_ROWS - dh, L), 0)
    va = jnp.concatenate([(vt.astype(F32) * a_r).astype(BF16),
                          jnp.where(sub == 0, a_r, 0.0).astype(BF16)], axis=0)
    cn_ref[hh, d] = decay * cn + _dot(va, kk)
    return m_new


def _scan_finish(ht_ref, og_ref, xc_ref, sz_ref, hn_ref, skip_ref, y_ref):
    L = SCAN_CHUNK
    dh = ML_HEAD_DIM

    def body(i, carry):
        rows = pl.ds(pl.multiple_of(i * L, L), L)
        for hh in range(SCAN_HEADS):
            hl = slice(dh * hh, dh * (hh + 1))
            h = og_ref[0, rows, hl].astype(F32) * ht_ref[hl, rows].T
            hn = h * lax.rsqrt(jnp.mean(h * h, axis=-1, keepdims=True) + EPS)
            y = ((hn * hn_ref[:, hl] + skip_ref[:, hl] * xc_ref[0, rows, hl].astype(F32))
                 * sz_ref[0, rows, hl].astype(F32))
            y_ref[0, rows, hl] = y.astype(BF16)
        return carry

    lax.fori_loop(0, ht_ref.shape[1] // L, body, 0, unroll=True)


def _ml_scan_kernel(*refs, ctx_out):
    (ql, kl, vtl, gcl, grl, ogl, xcl, szl, qc, kc, vtc, gcc, grc) = refs[:13]
    pos = 13
    if ctx_out:
        ogc, xcc, szc = refs[pos:pos + 3]
        pos += 3
    hn_ref, skip_ref = refs[pos:pos + 2]
    pos += 2
    yl_ref = refs[pos]
    pos += 1
    if ctx_out:
        yc_ref = refs[pos]
        pos += 1
    hl_ref, hc_ref, cn_ref = refs[pos:pos + 3]
    L = SCAN_CHUNK
    n_l = ql.shape[1] // L
    n_c = kc.shape[1] // L
    chains = [(hh, d) for hh in range(SCAN_HEADS) for d in range(2)]

    cn_ref[...] = jnp.zeros_like(cn_ref)
    hl_ref[...] = jnp.zeros_like(hl_ref)
    if ctx_out:
        hc_ref[...] = jnp.zeros_like(hc_ref)
    ms = [jnp.zeros((1, 1), F32) for _ in chains]
    for c in range(n_c):
        ms = [_scan_chunk(qc, kc, vtc, gcc, grc, hc_ref, cn_ref, m, (c if d == 0 else n_c - 1 - c) * L, hh, d,
                          ctx_out) for (hh, d), m in zip(chains, ms)]

    def body(i, carry):
        return tuple(
            _scan_chunk(ql, kl, vtl, gcl, grl, hl_ref, cn_ref, m,
                        pl.multiple_of((i if d == 0 else n_l - 1 - i) * L, L), hh, d, True)
            for (hh, d), m in zip(chains, carry))

    lax.fori_loop(0, n_l, body, tuple(ms), unroll=True)

    _scan_finish(hl_ref, ogl, xcl, szl, hn_ref, skip_ref, yl_ref)
    if ctx_out:
        _scan_finish(hc_ref, ogc, xcc, szc, hn_ref, skip_ref, yc_ref)


def _ml_scan(lat, ctx, head_norm, skip, ctx_out):
    B, T, C = lat["q"].shape
    Tc = ctx["q"].shape[1]
    wd = ML_HEAD_DIM * SCAN_HEADS

    def stream_specs(t, names):
        specs = []
        for nm in names:
            if nm == "gc":
                specs.append(pl.BlockSpec((1, SCAN_HEADS, t, GATE_LANES), lambda b, h: (b, h, 0, 0)))
            elif nm == "gr":
                specs.append(pl.BlockSpec((1, SCAN_HEADS * GATE_LANES, t), lambda b, h: (b, h, 0)))
            elif nm == "vt":
                specs.append(pl.BlockSpec((1, wd, t), lambda b, h: (b, h, 0)))
            else:
                specs.append(pl.BlockSpec((1, t, wd), lambda b, h: (b, 0, h)))
        return specs

    lat_names = ["q", "k", "vt", "gc", "gr", "og", "xc", "sz"]
    ctx_names = ["q", "k", "vt", "gc", "gr"] + (["og", "xc", "sz"] if ctx_out else [])
    head_vec = pl.BlockSpec((1, wd), lambda b, h: (0, h))
    out_specs = [pl.BlockSpec((1, T, wd), lambda b, h: (b, 0, h))]
    out_shape = [jax.ShapeDtypeStruct((B, T, C), BF16)]
    if ctx_out:
        out_specs.append(pl.BlockSpec((1, Tc, wd), lambda b, h: (b, 0, h)))
        out_shape.append(jax.ShapeDtypeStruct((B, Tc, C), BF16))
    outs = pl.pallas_call(
        functools.partial(_ml_scan_kernel, ctx_out=ctx_out),
        grid=(B, ML_HEADS // SCAN_HEADS),
        in_specs=stream_specs(T, lat_names) + stream_specs(Tc, ctx_names) + [head_vec, head_vec],
        out_specs=out_specs,
        out_shape=out_shape,
        scratch_shapes=[pltpu.VMEM((wd, T), F32), pltpu.VMEM((wd, Tc), F32),
                        pltpu.VMEM((SCAN_HEADS, 2, STATE_ROWS, ML_HEAD_DIM), F32)],
        compiler_params=_cparams(2),
        name="mlstm_scan",
    )(*[lat[nm] for nm in lat_names], *[ctx[nm] for nm in ctx_names], head_norm, skip)
    return (outs[0], outs[1]) if ctx_out else (outs[0], None)


def _swap_pairs(a):
    g = a.reshape(a.shape[:-1] + (2, 2, QK_ROPE // 4))
    return jnp.flip(g, axis=-2).reshape(a.shape)


def _prep_mla_weights(w_in, q_norm, kv_norm, w_uq, w_ukv, w_out):
    kr0 = Q_LORA + KV_LORA
    kr = w_in[:, kr0:kr0 + QK_ROPE]
    krs = _swap_pairs(kr)
    w_in_p = jnp.concatenate([w_in[:, :kr0], kr, krs, kr, krs, w_in[:, kr0 + QK_ROPE:]], axis=1)
    uq = w_uq.reshape(Q_LORA, MLA_HEADS, QK_HEAD)
    rope = uq[..., QK_NOPE:]
    uq_p = jnp.concatenate([uq[..., :QK_NOPE], rope, _swap_pairs(rope)], axis=-1).reshape(Q_LORA, MLA_QW)
    ukv = w_ukv.reshape(KV_LORA, MLA_HEADS, QK_NOPE + V_HEAD)
    uk_p = jnp.concatenate([ukv[..., :QK_NOPE], jnp.zeros((KV_LORA, MLA_HEADS, HEAD_PAD - QK_NOPE), w_ukv.dtype)],
                           axis=-1).reshape(KV_LORA, MLA_QW)
    uv = ukv[..., QK_NOPE:].reshape(KV_LORA, MLA_WIDTH)
    return {"w_in": w_in_p.astype(BF16), "q_norm": q_norm.reshape(1, -1), "kv_norm": kv_norm.reshape(1, -1),
            "w_uq": uq_p.astype(BF16), "w_uk": uk_p.astype(BF16), "w_uvt": uv.T.astype(BF16),
            "w_out": w_out.astype(BF16)}


def _rope_tables(n_tokens, rotate):
    if rotate:
        rows = n_tokens // GRID_W
        row = jnp.repeat(jnp.arange(rows, dtype=jnp.int32), GRID_W).astype(F32)
        col = jnp.tile(jnp.arange(GRID_W, dtype=jnp.int32), rows).astype(F32)
        qd = QK_ROPE // 4
        inv = ROPE_THETA ** (-jnp.arange(qd, dtype=F32) / qd)
        ang = jnp.stack([row[:, None] * inv, col[:, None] * inv], axis=1)
        cos, sin = jnp.cos(ang), jnp.sin(ang)
    else:
        cos = jnp.ones((n_tokens, 2, QK_ROPE // 4), F32)
        sin = jnp.zeros((n_tokens, 2, QK_ROPE // 4), F32)
    cf = jnp.stack([cos, cos], axis=2).reshape(n_tokens, QK_ROPE)
    sf = jnp.stack([-sin, sin], axis=2).reshape(n_tokens, QK_ROPE)
    tabq = (ATTN_SCALE * LOG2_E) * jnp.concatenate([jnp.ones((n_tokens, QK_NOPE), F32), cf, sf], axis=1)
    tabk = jnp.concatenate([cf, sf, cf, sf], axis=1)
    return tabq, tabk


def _dense_blockdiag(w):
    per = MXU_DIM // QKV_BLOCK
    wt = w.reshape(-1, per, QKV_BLOCK, QKV_BLOCK)
    eye = jnp.eye(per, dtype=w.dtype)
    dense = jnp.einsum('jncd,nm->jncmd', wt, eye)
    return dense.reshape(-1, MXU_DIM, MXU_DIM).astype(BF16)


def _prep_ml_weights(w_in, conv_w, conv_b, w_q, w_k, w_v, w_if, b_if, head_norm, skip, w_out, tm):
    H = ML_HEADS
    cols = jnp.stack([w_if[0, :, :H], w_if[0, :, H:], w_if[1, :, :H], w_if[1, :, H:]], axis=-1)
    cols = jnp.concatenate([cols, jnp.zeros(cols.shape[:2] + (GATE_LANES - 4,), cols.dtype)], axis=-1)
    w_if_p = jnp.concatenate([cols.reshape(cols.shape[0], H * GATE_LANES),
                              jnp.zeros((cols.shape[0], LANES - H * GATE_LANES), cols.dtype)], axis=1)
    bcols = jnp.stack([b_if[0, :H], b_if[0, H:], b_if[1, :H], b_if[1, H:]], axis=-1)
    bcols = jnp.concatenate([bcols, jnp.zeros((H, GATE_LANES - 4), bcols.dtype)], axis=-1).reshape(1, -1)
    b_if_p = jnp.concatenate([bcols, jnp.zeros((1, LANES - H * GATE_LANES), bcols.dtype)], axis=1)
    t = jnp.arange(tm)
    same = (t[:, None] // SCAN_CHUNK) == (t[None, :] // SCAN_CHUNK)
    tri_f = (same & (t[None, :] <= t[:, None])).astype(BF16)
    tri_b = (same & (t[None, :] >= t[:, None])).astype(BF16)
    return {"w_in": w_in.astype(BF16), "conv_w": conv_w, "conv_b": conv_b.reshape(1, -1),
            "w_q": _dense_blockdiag(w_q), "w_k": _dense_blockdiag(w_k), "w_v": _dense_blockdiag(w_v),
            "w_vt": jnp.swapaxes(_dense_blockdiag(w_v), 1, 2),
            "w_if": w_if_p.astype(BF16), "b_if": b_if_p, "tri_f": tri_f, "tri_b": tri_b,
            "head_norm": head_norm.reshape(1, -1), "skip": skip.reshape(1, -1), "w_out": w_out.astype(BF16)}


def _row_tile(t, want):
    tm = min(t, want)
    assert t % tm == 0, (t, tm)
    return tm


def kernel(x, c, ctx, c_ctx, ada_w, ada_b, norm_pre, norm_post, mla_w_in, mla_q_norm, mla_kv_norm, mla_w_uq,
           mla_w_ukv, mla_w_out, ml_w_in, ml_conv_w, ml_conv_b, ml_w_q, ml_w_k, ml_w_v, ml_w_if, ml_b_if,
           ml_head_norm, ml_skip, ml_w_out):
    B, T, D = x.shape
    Tc = ctx.shape[1]
    assert D == D_MODEL and T % GRID_W == 0 and T % SCAN_CHUNK == 0 and Tc % SCAN_CHUNK == 0
    tm_l = _row_tile(T, 512)
    tm_c = _row_tile(Tc, 256)
    tf_l = _row_tile(T, 256)
    tf_c = _row_tile(Tc, 256)
    tq_l = _row_tile(T, 1024)
    tq_c = _row_tile(Tc, 256)

    n_rows = -(-(B + 1) // 8) * 8
    cc = jnp.concatenate([c, c_ctx[None, :], jnp.zeros((n_rows - B - 1, D), F32)], axis=0)
    mods = _modulation(cc, ada_w, ada_b)

    tabq_l, tabk_l = _rope_tables(T, True)
    tabq_c, tabk_c = _rope_tables(Tc, False)

    for i in range(DEPTH):
        need_ctx_out = i < DEPTH - 1
        j = i // 2
        m_l = mods[i, :B]
        m_c = jnp.broadcast_to(mods[i, B:B + 1], (B, 3 * D))
        mod_l = jnp.stack([m_l[:, :D], m_l[:, D:2 * D]], axis=1)
        mod_c = jnp.stack([m_c[:, :D], m_c[:, D:2 * D]], axis=1)
        gate_l = m_l[:, None, 2 * D:]
        gate_c = m_c[:, None, 2 * D:]
        npre = norm_pre[i].reshape(1, D)
        npost = norm_post[i].reshape(1, D)
        if i % 2 == 0:
            w = _prep_mla_weights(mla_w_in[j], mla_q_norm[j], mla_kv_norm[j], mla_w_uq[j], mla_w_ukv[j],
                                  mla_w_out[j])
            ql, kl, vtl, sgl = _mla_project(x, mod_l, npre, w, tabq_l, tabk_l, tm_l)
            qc, kc, vtc, sgc = _mla_project(ctx, mod_c, npre, w, tabq_c, tabk_c, tm_c)
            al = _attention(ql, [kc, kl], [vtc, vtl], sgl, tq_l)
            ac = _attention(qc, [kc], [vtc], sgc, tq_c) if need_ctx_out else None
            w_out = w["w_out"]
        else:
            w = _prep_ml_weights(ml_w_in[j], ml_conv_w[j], ml_conv_b[j], ml_w_q[j], ml_w_k[j], ml_w_v[j],
                                 ml_w_if[j], ml_b_if[j], ml_head_norm[j], ml_skip[j], ml_w_out[j], tf_l)
            assert tf_l == tf_c
            streams = []
            for xs, mod, tf in ((x, mod_l, tf_l), (ctx, mod_c, tf_c)):
                q, k, vt, xc, sz, og, gc, gr = _ml_front(xs, mod, npre, w, tf)
                streams.append({"q": q, "k": k, "vt": vt, "gc": gc, "gr": gr, "og": og, "xc": xc, "sz": sz})
            al, ac = _ml_scan(streams[0], streams[1], w["head_norm"], w["skip"], need_ctx_out)
            w_out = w["w_out"]
        x = _out_project(al, w_out, x, gate_l, npost, tm_l)
        if need_ctx_out:
            ctx = _out_project(ac, w_out, ctx, gate_c, npost, tm_c)
    return x
```

```python
import functools

import jax
import jax.numpy as jnp
from jax import lax
from jax.experimental import pallas as pl
from jax.experimental.pallas import tpu as pltpu

F32 = jnp.float32
BF16 = jnp.bfloat16

D_MODEL = 1024
DEPTH = 4
GRID_W = 64
EPS = 1e-6
ROPE_THETA = 10000.0
MLA_HEADS = 16
QK_NOPE = 64
QK_ROPE = 32
V_HEAD = 64
Q_LORA = 256
KV_LORA = 256
QK_HEAD = QK_NOPE + QK_ROPE
MLA_WIDTH = MLA_HEADS * V_HEAD
ATTN_SCALE = QK_HEAD ** -0.5
LOG2_E = 1.4426950408889634
ML_INNER = 2 * D_MODEL
ML_HEADS = 8
ML_HEAD_DIM = ML_INNER // ML_HEADS
QKV_BLOCK = 4
CONV_K = 5

LANES = 128
MXU_DIM = 256
VMEM_LIMIT = 56 * 1024 * 1024

HEAD_PAD = LANES
MLA_QW = MLA_HEADS * HEAD_PAD
SCAN_CHUNK = MXU_DIM
HALO = 16
GATE_LANES = 8
SCAN_HEADS = 2

NT_DIMS = (((1,), (1,)), ((), ()))


def _cparams(n_axes):
    return pltpu.CompilerParams(dimension_semantics=("parallel",) * n_axes,
                                vmem_limit_bytes=VMEM_LIMIT)


def _resident_spec(shape):
    nd = len(shape)
    return pl.BlockSpec(shape, lambda *_: (0,) * nd, pipeline_mode=pl.Buffered(1))


def _layer_spec(shape, layer, resident=False):
    nd = len(shape)
    kw = {"pipeline_mode": pl.Buffered(1)} if resident else {}
    return pl.BlockSpec((None,) + tuple(shape), lambda *_: (layer,) + (0,) * nd, **kw)


def _mod_spec(layer, row=None):
    if row is None:
        return pl.BlockSpec((None, 1, 3, D_MODEL), lambda b, i: (layer, b, 0, 0))
    return pl.BlockSpec((None, 1, 3, D_MODEL), lambda b, i: (layer, row, 0, 0))


def _rms(x, g):
    r = lax.rsqrt(jnp.mean(x * x, axis=-1, keepdims=True) + EPS)
    return (x * r) * g


def _sigmoid(x):
    return 0.5 * jnp.tanh(0.5 * x) + 0.5


def _silu(x):
    return x * _sigmoid(x)


def _split3(x):
    x1 = x.astype(BF16)
    r1 = x - x1.astype(F32)
    x2 = r1.astype(BF16)
    x3 = (r1 - x2.astype(F32)).astype(BF16)
    return x1, x2, x3


def _dot(a, b):
    return jnp.dot(a, b, preferred_element_type=F32)


def _mod_kernel(c_ref, w_ref, b_ref, o_ref):
    a1, a2, a3 = _split3(_silu(c_ref[...]))
    w = w_ref[0]
    w1 = w.astype(BF16)
    w2 = (w - w1.astype(F32)).astype(BF16)
    acc = _dot(a1, w1) + (_dot(a1, w2) + _dot(a2, w1)) + (_dot(a2, w2) + _dot(a3, w1))
    o_ref[0] = acc + b_ref[0]


def _modulation(cc, ada_w, ada_b):
    rows = cc.shape[0]
    nblk = ada_w.shape[2] // D_MODEL
    return pl.pallas_call(
        _mod_kernel,
        grid=(DEPTH, nblk),
        in_specs=[pl.BlockSpec((rows, D_MODEL), lambda i, j: (0, 0)),
                  pl.BlockSpec((1, D_MODEL, D_MODEL), lambda i, j: (i, 0, j)),
                  pl.BlockSpec((1, 1, D_MODEL), lambda i, j: (i, 0, j))],
        out_specs=pl.BlockSpec((1, rows, D_MODEL), lambda i, j: (i, 0, j)),
        out_shape=jax.ShapeDtypeStruct((DEPTH, rows, ada_w.shape[2]), F32),
        compiler_params=_cparams(2),
        name="modulation",
    )(cc, ada_w, ada_b.reshape(DEPTH, 1, -1))


def _prenorm(x_ref, mod_ref, npre_ref):
    h = _rms(x_ref[0], npre_ref[...]) * (1.0 + mod_ref[0, 1:2, :]) + mod_ref[0, 0:1, :]
    return h.astype(BF16)


def _mla_proj_kernel(x_ref, mod_ref, npre_ref, win_ref, qn_ref, kvn_ref, wuq_ref, wuk_ref, wuvt_ref,
                     tabq_ref, tabk_ref, q_ref, k_ref, vt_ref, sg_ref):
    h = _prenorm(x_ref, mod_ref, npre_ref)
    u = _dot(h, win_ref[...])
    cq = u[:, :Q_LORA]
    ckv = u[:, Q_LORA:Q_LORA + KV_LORA]
    krp = u[:, Q_LORA + KV_LORA:Q_LORA + KV_LORA + LANES]
    gate = u[:, Q_LORA + KV_LORA + LANES:]
    ckvn = _rms(ckv, kvn_ref[...]).astype(BF16)
    q = _dot(_rms(cq, qn_ref[...]).astype(BF16), wuq_ref[...])
    kn = _dot(ckvn, wuk_ref[...])
    vt_ref[0] = lax.dot_general(wuvt_ref[...], ckvn, NT_DIMS, preferred_element_type=F32).astype(BF16)
    t = krp * tabk_ref[...]
    r = t + pltpu.roll(t, QK_ROPE, 1)
    lane = lax.broadcasted_iota(jnp.int32, r.shape, 1)
    rk = jnp.where(lane >= QK_NOPE, r, 0.0)
    tabq = tabq_ref[...]
    for hh in range(MLA_HEADS):
        sl = slice(HEAD_PAD * hh, HEAD_PAD * (hh + 1))
        q_ref[0, :, sl] = (q[:, sl] * tabq).astype(BF16)
        k_ref[0, :, sl] = (kn[:, sl] + rk).astype(BF16)
    sg_ref[0] = _silu(gate).astype(BF16)


def _mla_project(x, mods, mod_row, npre, w, layer, j, tabq, tabk, tm):
    B, T, D = x.shape
    nin = w["w_in"].shape[-1]
    row = lambda width: pl.BlockSpec((1, tm, width), lambda b, i: (b, i, 0))
    return pl.pallas_call(
        _mla_proj_kernel,
        grid=(B, T // tm),
        in_specs=[row(D),
                  _mod_spec(layer, mod_row),
                  _layer_spec((1, D), layer),
                  _layer_spec((D, nin), j),
                  _layer_spec((1, Q_LORA), j),
                  _layer_spec((1, KV_LORA), j),
                  _layer_spec((Q_LORA, MLA_QW), j),
                  _layer_spec((KV_LORA, MLA_QW), j),
                  _layer_spec((MLA_WIDTH, KV_LORA), j),
                  pl.BlockSpec((tm, LANES), lambda b, i: (i, 0)),
                  pl.BlockSpec((tm, LANES), lambda b, i: (i, 0))],
        out_specs=[row(MLA_QW), row(MLA_QW),
                   pl.BlockSpec((1, MLA_WIDTH, tm), lambda b, i: (b, 0, i)),
                   row(MLA_WIDTH)],
        out_shape=[jax.ShapeDtypeStruct((B, T, MLA_QW), BF16),
                   jax.ShapeDtypeStruct((B, T, MLA_QW), BF16),
                   jax.ShapeDtypeStruct((B, MLA_WIDTH, T), BF16),
                   jax.ShapeDtypeStruct((B, T, MLA_WIDTH), BF16)],
        compiler_params=_cparams(2),
        name="mla_project",
    )(x, mods, npre, w["w_in"], w["q_norm"], w["kv_norm"], w["w_uq"], w["w_uk"], w["w_uvt"], tabq, tabk)


VAUG_ROWS = V_HEAD + 16


def _attn_kernel(*refs, n_seg, tq, n_pairs):
    q_ref = refs[0]
    k_refs = refs[1:1 + n_seg]
    vt_refs = refs[1 + n_seg:1 + 2 * n_seg]
    sg_ref = refs[1 + 2 * n_seg]
    o_ref = refs[2 + 2 * n_seg]
    va_refs = refs[3 + 2 * n_seg:3 + 3 * n_seg]
    ot_ref = refs[3 + 3 * n_seg]
    n_q = q_ref.shape[1] // tq

    for vt_ref, va_ref in zip(vt_refs, va_refs):
        tk = vt_ref.shape[2]
        ones_row = jnp.where(lax.broadcasted_iota(jnp.int32, (VAUG_ROWS - V_HEAD, tk), 0) == 0, 1.0, 0.0).astype(BF16)
        for h in range(2 * n_pairs):
            va_ref[h, :V_HEAD, :] = vt_ref[0, V_HEAD * h:V_HEAD * (h + 1), :]
            va_ref[h, V_HEAD:, :] = ones_row

    units = [(p, j, hh) for p in range(n_pairs) for j in range(n_q) for hh in range(2)]

    def scores(u):
        p, j, hh = u
        rows = slice(j * tq, (j + 1) * tq)
        sl = slice(HEAD_PAD * (2 * p + hh), HEAD_PAD * (2 * p + hh + 1))
        qb = q_ref[0, rows, sl]
        ss, m = [], None
        for k in k_refs:
            s = lax.dot_general(k[0, :, sl], qb, NT_DIMS, preferred_element_type=F32)
            ss.append(s)
            mm = jnp.max(s, axis=0, keepdims=True)
            m = mm if m is None else jnp.maximum(m, mm)
        return ss, m

    def values(u, sm):
        p, j, hh = u
        rows = slice(j * tq, (j + 1) * tq)
        ss, m = sm
        oa = None
        for s, va in zip(ss, va_refs):
            t = _dot(va[2 * p + hh], jnp.exp2(s - m).astype(BF16))
            oa = t if oa is None else oa + t
        ot_ref[V_HEAD * hh:V_HEAD * (hh + 1), :] = oa[:V_HEAD] * (1.0 / oa[V_HEAD:V_HEAD + 1])
        if hh == 1:
            cols = slice(2 * V_HEAD * p, 2 * V_HEAD * (p + 1))
            o_ref[0, rows, cols] = (ot_ref[...].T * sg_ref[0, rows, cols].astype(F32)).astype(BF16)

    prev = scores(units[0])
    for u_prev, u in zip(units[:-1], units[1:]):
        cur = scores(u)
        values(u_prev, prev)
        prev = cur
    values(units[-1], prev)


def _attention(q, ks, vts, sg, tq, n_pairs):
    B, Tq, _ = q.shape
    n_seg = len(ks)
    blk = lambda t, width: pl.BlockSpec((1, t, width * n_pairs), lambda b, p: (b, 0, p))
    return pl.pallas_call(
        functools.partial(_attn_kernel, n_seg=n_seg, tq=tq, n_pairs=n_pairs),
        grid=(B, MLA_HEADS // (2 * n_pairs)),
        in_specs=([blk(Tq, 2 * HEAD_PAD)]
                  + [blk(k.shape[1], 2 * HEAD_PAD) for k in ks]
                  + [pl.BlockSpec((1, 2 * V_HEAD * n_pairs, vt.shape[2]), lambda b, p: (b, p, 0)) for vt in vts]
                  + [blk(Tq, 2 * V_HEAD)]),
        out_specs=blk(Tq, 2 * V_HEAD),
        out_shape=jax.ShapeDtypeStruct((B, Tq, MLA_WIDTH), BF16),
        scratch_shapes=([pltpu.VMEM((2 * n_pairs, VAUG_ROWS, vt.shape[2]), BF16) for vt in vts]
                        + [pltpu.VMEM((2 * V_HEAD, tq), F32)]),
        compiler_params=_cparams(2),
        name="mla_attention",
    )(q, *ks, *vts, sg)


def _out_kernel(a_ref, w_ref, x_ref, mod_ref, npost_ref, o_ref):
    y = _dot(a_ref[0], w_ref[...])
    o_ref[0] = x_ref[0] + mod_ref[0, 2:3, :] * _rms(y, npost_ref[...])


def _out_project(a, w_out, j, x, mods, mod_row, npost, layer, tm):
    B, T, K = a.shape
    D = x.shape[-1]
    return pl.pallas_call(
        _out_kernel,
        grid=(B, T // tm),
        in_specs=[pl.BlockSpec((1, tm, K), lambda b, i: (b, i, 0)),
                  _layer_spec((K, D), j),
                  pl.BlockSpec((1, tm, D), lambda b, i: (b, i, 0)),
                  _mod_spec(layer, mod_row),
                  _layer_spec((1, D), layer)],
        out_specs=pl.BlockSpec((1, tm, D), lambda b, i: (b, i, 0)),
        out_shape=jax.ShapeDtypeStruct(x.shape, F32),
        compiler_params=_cparams(2),
        name="out_project",
    )(a, w_out, x, mods, npost)


def _ml_front_kernel(x_ref, xp_ref, xn_ref, mod_ref, npre_ref, win_ref, cw_ref, cb_ref, wq_ref, wk_ref,
                     wvt_ref, wif_ref, bif_ref, trif_ref, trib_ref,
                     q_ref, k_ref, vt_ref, xc_ref, sz_ref, og_ref, gc_ref, gr_ref, xe_ref, *, tm):
    i = pl.program_id(1)
    last = pl.num_programs(1) - 1
    pad = CONV_K // 2
    half = HALO // 2

    def prenorm(xv):
        return _rms(xv, npre_ref[...]) * (1.0 + mod_ref[0, 1:2, :]) + mod_ref[0, 0:1, :]

    hx = jnp.concatenate([prenorm(xp_ref[0]), prenorm(x_ref[0]), prenorm(xn_ref[0])], axis=0).astype(BF16)
    xme = _dot(hx, win_ref[:, :ML_INNER])
    xe_ref[0:half, :] = jnp.where(i > 0, xme[0:half], 0.0)
    xe_ref[half:half + tm, :] = xme[half:half + tm]
    xe_ref[half + tm:, :] = jnp.where(i < last, xme[half + tm:], 0.0)
    h = hx[half:half + tm]
    xm = xme[half:half + tm].astype(BF16)
    sz_ref[0] = _silu(_dot(h, win_ref[:, ML_INNER:2 * ML_INNER])).astype(BF16)
    og_ref[0] = _sigmoid(_dot(h, win_ref[:, 2 * ML_INNER:])).astype(BF16)

    xe = xe_ref[...]
    acc = cb_ref[...]
    for j in range(CONV_K):
        sh = (pad - j) % (tm + HALO)
        xs = xe if sh == 0 else pltpu.roll(xe, sh, 0)
        acc = acc + cw_ref[j:j + 1, :] * xs[half:half + tm, :]
    xc = _silu(acc)
    xcb = xc.astype(BF16)
    xc_ref[0] = xcb

    n_tiles = ML_INNER // MXU_DIM
    n_gate = ML_HEADS * GATE_LANES
    gt = jnp.zeros((n_gate, tm), F32) + bif_ref[...]
    for j in range(n_tiles):
        sl = slice(MXU_DIM * j, MXU_DIM * (j + 1))
        qj = _dot(xcb[:, sl], wq_ref[j]).astype(BF16)
        kj = _dot(xcb[:, sl], wk_ref[j])
        vtj = lax.dot_general(wvt_ref[j], xm[:, sl], NT_DIMS, preferred_element_type=F32).astype(BF16)
        q_ref[0, :, sl] = qj
        k_ref[0, :, sl] = (kj * (ML_HEAD_DIM ** -0.5)).astype(BF16)
        vt_ref[0, sl, :] = vtj
        gt = gt + lax.dot_general(wif_ref[:, sl], qj, NT_DIMS, preferred_element_type=F32)
        gt = gt + lax.dot_general(wif_ref[:, ML_INNER + MXU_DIM * j:ML_INNER + MXU_DIM * (j + 1)], kj.astype(BF16),
                                  NT_DIMS, preferred_element_type=F32)
        gt = gt + _dot(wif_ref[:, 2 * ML_INNER + MXU_DIM * j:2 * ML_INNER + MXU_DIM * (j + 1)], vtj)

    sub = lax.broadcasted_iota(jnp.int32, gt.shape, 0) % GATE_LANES
    is_f_fwd = sub == 1
    is_f_bwd = sub == 3
    logsig = jnp.minimum(gt, 0.0) - jnp.log(1.0 + jnp.exp(-jnp.abs(gt)))
    gs = jnp.where(is_f_fwd | is_f_bwd, logsig, gt)
    g1, g2, g3 = _split3(gs)
    cum_f = _dot(g1, trib_ref[...]) + _dot(g2, trib_ref[...]) + _dot(g3, trib_ref[...])
    cum_b = _dot(g1, trif_ref[...]) + _dot(g2, trif_ref[...]) + _dot(g3, trif_ref[...])
    out_t = jnp.where(is_f_fwd, cum_f, jnp.where(is_f_bwd, cum_b, gs))
    gr_ref[0] = out_t
    out = jnp.concatenate([out_t, jnp.zeros((LANES - n_gate, tm), F32)], axis=0).T
    diff = out - pltpu.roll(out, LANES - 1, 1)
    for hd in range(ML_HEADS):
        gc_ref[0, hd] = diff[:, GATE_LANES * hd:GATE_LANES * (hd + 1)]


def _ml_front(x, mods, mod_row, npre, w, layer, j, tm):
    B, T, D = x.shape
    C = ML_INNER
    half = HALO // 2
    nh = tm // half
    n_halo = T // half
    row = lambda width: pl.BlockSpec((1, tm, width), lambda b, i: (b, i, 0))
    n_tiles = C // MXU_DIM
    n_gate = ML_HEADS * GATE_LANES
    tile_w = _layer_spec((n_tiles, MXU_DIM, MXU_DIM), j, resident=True)
    return pl.pallas_call(
        functools.partial(_ml_front_kernel, tm=tm),
        grid=(B, T // tm),
        in_specs=[row(D),
                  pl.BlockSpec((1, half, D), lambda b, i: (b, jnp.maximum(i * nh - 1, 0), 0)),
                  pl.BlockSpec((1, half, D), lambda b, i: (b, jnp.minimum((i + 1) * nh, n_halo - 1), 0)),
                  _mod_spec(layer, mod_row),
                  _layer_spec((1, D), layer, resident=True),
                  _layer_spec((D, 3 * C), j, resident=True),
                  _layer_spec((CONV_K, C), j, resident=True),
                  _layer_spec((1, C), j, resident=True),
                  tile_w, tile_w, tile_w,
                  _layer_spec((n_gate, 3 * C), j, resident=True),
                  _layer_spec((n_gate, 1), j, resident=True),
                  _resident_spec((tm, tm)),
                  _resident_spec((tm, tm))],
        out_specs=[row(C), row(C),
                   pl.BlockSpec((1, C, tm), lambda b, i: (b, 0, i)),
                   row(C), row(C), row(C),
                   pl.BlockSpec((1, ML_HEADS, tm, GATE_LANES), lambda b, i: (b, 0, i, 0)),
                   pl.BlockSpec((1, n_gate, tm), lambda b, i: (b, 0, i))],
        out_shape=[jax.ShapeDtypeStruct((B, T, C), BF16),
                   jax.ShapeDtypeStruct((B, T, C), BF16),
                   jax.ShapeDtypeStruct((B, C, T), BF16),
                   jax.ShapeDtypeStruct((B, T, C), BF16),
                   jax.ShapeDtypeStruct((B, T, C), BF16),
                   jax.ShapeDtypeStruct((B, T, C), BF16),
                   jax.ShapeDtypeStruct((B, ML_HEADS, T, GATE_LANES), F32),
                   jax.ShapeDtypeStruct((B, n_gate, T), F32)],
        scratch_shapes=[pltpu.VMEM((tm + HALO, C), F32)],
        compiler_params=_cparams(2),
        name="mlstm_front",
    )(x, x, x, mods, npre, w["w_in"], w["conv_w"], w["conv_b"], w["w_q"], w["w_k"], w["w_vt"],
      w["w_if"], w["b_if"], w["tri_f"], w["tri_b"])


STATE_ROWS = ML_HEAD_DIM + 16


def _scan_chunk(q_ref, k_ref, vt_ref, gc_ref, gr_ref, ht_ref, cn_ref, m, c, hh, d, with_out, first_visit):
    L = SCAN_CHUNK
    dh = ML_HEAD_DIM
    rows = slice(c * L, (c + 1) * L)
    hl = slice(dh * hh, dh * (hh + 1))
    kk = k_ref[0, rows, hl]
    vt = vt_ref[0, hl, rows]
    g0 = GATE_LANES * hh + 2 * d
    ig_r = gr_ref[0, g0:g0 + 1, rows]
    b_r = gr_ref[0, g0 + 1:g0 + 2, rows]
    b_end = b_r[:, L - 1:L] if d == 0 else b_r[:, 0:1]
    w_r = b_end - b_r + ig_r
    m_new = jnp.maximum(b_end + m, jnp.max(w_r, axis=-1, keepdims=True))
    a_r = jnp.exp(w_r - m_new)
    decay = jnp.exp(b_end + m - m_new)
    cn = cn_ref[hh, d]
    if with_out:
        qq = q_ref[0, rows, hl]
        u_c = gc_ref[0, hh, rows, 2 * d:2 * d + 1]
        si = lax.broadcasted_iota(jnp.int32, (L, L), 0)
        ti = lax.broadcasted_iota(jnp.int32, (L, L), 1)
        mask = (si <= ti) if d == 0 else (si >= ti)
        log_d = jnp.where(mask, u_c + b_r, -jnp.inf)
        m_t = jnp.maximum(b_r + m, jnp.max(log_d, axis=0, keepdims=True))
        inter = jnp.exp(b_r + m - m_t)
        s = lax.dot_general(kk, qq, NT_DIMS, preferred_element_type=F32) * jnp.exp(log_d - m_t)
        cq = lax.dot_general(cn.astype(BF16), qq, NT_DIMS, preferred_element_type=F32)
        num = _dot(vt, s.astype(BF16)) + inter * cq[:dh]
        den = jnp.sum(s, axis=0, keepdims=True) + inter * cq[dh:dh + 1]
        hh_t = num * (1.0 / jnp.maximum(jnp.abs(den), jnp.exp(-m_t)))
        if first_visit:
            ht_ref[hl, rows] = hh_t
        else:
            ht_ref[hl, rows] += hh_t
    sub = lax.broadcasted_iota(jnp.int32, (STATE_ROWS - dh, L), 0)
    va = jnp.concatenate([vt * a_r.astype(BF16),
                          jnp.where(sub == 0, a_r, 0.0).astype(BF16)], axis=0)
    cn_ref[hh, d] = decay * cn + _dot(va, kk)
    return m_new


def _scan_finish(ht_ref, og_ref, xc_ref, sz_ref, hn_ref, skip_ref, y_ref):
    L = SCAN_CHUNK
    dh = ML_HEAD_DIM

    def body(i, carry):
        rows = pl.ds(pl.multiple_of(i * L, L), L)
        for hh in range(SCAN_HEADS):
            hl = slice(dh * hh, dh * (hh + 1))
            h = og_ref[0, rows, hl].astype(F32) * ht_ref[hl, rows].T
            hn = h * lax.rsqrt(jnp.mean(h * h, axis=-1, keepdims=True) + EPS)
            y = ((hn * hn_ref[:, hl] + skip_ref[:, hl] * xc_ref[0, rows, hl].astype(F32))
                 * sz_ref[0, rows, hl].astype(F32))
            y_ref[0, rows, hl] = y.astype(BF16)
        return carry

    lax.fori_loop(0, ht_ref.shape[1] // L, body, 0, unroll=True)


def _ml_scan_kernel(*refs, ctx_out):
    (ql, kl, vtl, gcl, grl, ogl, xcl, szl, qc, kc, vtc, gcc, grc) = refs[:13]
    pos = 13
    if ctx_out:
        ogc, xcc, szc = refs[pos:pos + 3]
        pos += 3
    hn_ref, skip_ref = refs[pos:pos + 2]
    pos += 2
    yl_ref = refs[pos]
    pos += 1
    if ctx_out:
        yc_ref = refs[pos]
        pos += 1
    hl_ref, hc_ref, cn_ref = refs[pos:pos + 3]
    L = SCAN_CHUNK
    n_l = ql.shape[1] // L
    n_c = kc.shape[1] // L
    chains = [(hh, d) for hh in range(SCAN_HEADS) for d in range(2)]

    cn_ref[...] = jnp.zeros_like(cn_ref)
    ms = [jnp.zeros((1, 1), F32) for _ in chains]

    def sweep(refs, h_ref, n, with_out, ms):
        for i in range(n):
            new_ms = []
            for (hh, d), m in zip(chains, ms):
                c = i if d == 0 else n - 1 - i
                first = (2 * i + 1 < n) or (2 * i + 1 == n and d == 0)
                new_ms.append(_scan_chunk(*refs, h_ref, cn_ref, m, c, hh, d, with_out, first))
            ms = new_ms
        return ms

    ms = sweep((qc, kc, vtc, gcc, grc), hc_ref, n_c, ctx_out, ms)
    sweep((ql, kl, vtl, gcl, grl), hl_ref, n_l, True, ms)

    _scan_finish(hl_ref, ogl, xcl, szl, hn_ref, skip_ref, yl_ref)
    if ctx_out:
        _scan_finish(hc_ref, ogc, xcc, szc, hn_ref, skip_ref, yc_ref)


def _ml_scan(lat, ctx, head_norm, skip, j, ctx_out):
    B, T, C = lat["q"].shape
    Tc = ctx["q"].shape[1]
    wd = ML_HEAD_DIM * SCAN_HEADS

    def stream_specs(t, names):
        specs = []
        for nm in names:
            if nm == "gc":
                specs.append(pl.BlockSpec((1, SCAN_HEADS, t, GATE_LANES), lambda b, h: (b, h, 0, 0)))
            elif nm == "gr":
                specs.append(pl.BlockSpec((1, SCAN_HEADS * GATE_LANES, t), lambda b, h: (b, h, 0)))
            elif nm == "vt":
                specs.append(pl.BlockSpec((1, wd, t), lambda b, h: (b, h, 0)))
            else:
                specs.append(pl.BlockSpec((1, t, wd), lambda b, h: (b, 0, h)))
        return specs

    lat_names = ["q", "k", "vt", "gc", "gr", "og", "xc", "sz"]
    ctx_names = ["q", "k", "vt", "gc", "gr"] + (["og", "xc", "sz"] if ctx_out else [])
    head_vec = pl.BlockSpec((None, 1, wd), lambda b, h: (j, 0, h))
    out_specs = [pl.BlockSpec((1, T, wd), lambda b, h: (b, 0, h))]
    out_shape = [jax.ShapeDtypeStruct((B, T, C), BF16)]
    if ctx_out:
        out_specs.append(pl.BlockSpec((1, Tc, wd), lambda b, h: (b, 0, h)))
        out_shape.append(jax.ShapeDtypeStruct((B, Tc, C), BF16))
    outs = pl.pallas_call(
        functools.partial(_ml_scan_kernel, ctx_out=ctx_out),
        grid=(B, ML_HEADS // SCAN_HEADS),
        in_specs=stream_specs(T, lat_names) + stream_specs(Tc, ctx_names) + [head_vec, head_vec],
        out_specs=out_specs,
        out_shape=out_shape,
        scratch_shapes=[pltpu.VMEM((wd, T), F32), pltpu.VMEM((wd, Tc), F32),
                        pltpu.VMEM((SCAN_HEADS, 2, STATE_ROWS, ML_HEAD_DIM), F32)],
        compiler_params=_cparams(2),
        name="mlstm_scan",
    )(*[lat[nm] for nm in lat_names], *[ctx[nm] for nm in ctx_names], head_norm, skip)
    return (outs[0], outs[1]) if ctx_out else (outs[0], None)


def _swap_pairs(a):
    g = a.reshape(a.shape[:-1] + (2, 2, QK_ROPE // 4))
    return jnp.flip(g, axis=-2).reshape(a.shape)


def _prep_mla_weights(w_in, q_norm, kv_norm, w_uq, w_ukv, w_out):
    A = w_in.shape[0]
    kr0 = Q_LORA + KV_LORA
    kr = w_in[..., kr0:kr0 + QK_ROPE]
    krs = _swap_pairs(kr)
    w_in_p = jnp.concatenate([w_in[..., :kr0], kr, krs, kr, krs, w_in[..., kr0 + QK_ROPE:]], axis=-1)
    uq = w_uq.reshape(A, Q_LORA, MLA_HEADS, QK_HEAD)
    rope = uq[..., QK_NOPE:]
    uq_p = jnp.concatenate([uq[..., :QK_NOPE], rope, _swap_pairs(rope)], axis=-1).reshape(A, Q_LORA, MLA_QW)
    ukv = w_ukv.reshape(A, KV_LORA, MLA_HEADS, QK_NOPE + V_HEAD)
    uk_p = jnp.concatenate([ukv[..., :QK_NOPE], jnp.zeros((A, KV_LORA, MLA_HEADS, HEAD_PAD - QK_NOPE), w_ukv.dtype)],
                           axis=-1).reshape(A, KV_LORA, MLA_QW)
    uvt = jnp.swapaxes(ukv[..., QK_NOPE:].reshape(A, KV_LORA, MLA_WIDTH), 1, 2)
    return {"w_in": w_in_p.astype(BF16), "q_norm": q_norm[:, None, :], "kv_norm": kv_norm[:, None, :],
            "w_uq": uq_p.astype(BF16), "w_uk": uk_p.astype(BF16), "w_uvt": uvt.astype(BF16),
            "w_out": w_out.astype(BF16)}


def _rope_tables(n_tokens, rotate):
    if rotate:
        rows = n_tokens // GRID_W
        row = jnp.repeat(jnp.arange(rows, dtype=jnp.int32), GRID_W).astype(F32)
        col = jnp.tile(jnp.arange(GRID_W, dtype=jnp.int32), rows).astype(F32)
        qd = QK_ROPE // 4
        inv = ROPE_THETA ** (-jnp.arange(qd, dtype=F32) / qd)
        ang = jnp.stack([row[:, None] * inv, col[:, None] * inv], axis=1)
        cos, sin = jnp.cos(ang), jnp.sin(ang)
    else:
        cos = jnp.ones((n_tokens, 2, QK_ROPE // 4), F32)
        sin = jnp.zeros((n_tokens, 2, QK_ROPE // 4), F32)
    cf = jnp.stack([cos, cos], axis=2).reshape(n_tokens, QK_ROPE)
    sf = jnp.stack([-sin, sin], axis=2).reshape(n_tokens, QK_ROPE)
    tabq = (ATTN_SCALE * LOG2_E) * jnp.concatenate([jnp.ones((n_tokens, QK_NOPE), F32), cf, sf], axis=1)
    tabk = jnp.concatenate([cf, sf, cf, sf], axis=1)
    return tabq, tabk


def _dense_blockdiag(w, transpose=False):
    per = MXU_DIM // QKV_BLOCK
    wt = w.reshape(w.shape[0], -1, per, QKV_BLOCK, QKV_BLOCK)
    eye = jnp.eye(per, dtype=w.dtype)
    dense = jnp.einsum('ljndc,nm->ljncmd' if transpose else 'ljncd,nm->ljncmd', wt, eye)
    return dense.reshape(w.shape[0], -1, MXU_DIM, MXU_DIM).astype(BF16)


def _prep_ml_weights(w_in, conv_w, conv_b, w_q, w_k, w_v, w_if, b_if, head_norm, skip, w_out, tm):
    H = ML_HEADS
    nl = w_in.shape[0]
    cols = jnp.stack([w_if[:, 0, :, :H], w_if[:, 0, :, H:], w_if[:, 1, :, :H], w_if[:, 1, :, H:]], axis=-1)
    cols = jnp.concatenate([cols, jnp.zeros(cols.shape[:3] + (GATE_LANES - 4,), cols.dtype)], axis=-1)
    w_if_t = jnp.swapaxes(cols.reshape(nl, cols.shape[1], H * GATE_LANES), 1, 2)
    bcols = jnp.stack([b_if[:, 0, :H], b_if[:, 0, H:], b_if[:, 1, :H], b_if[:, 1, H:]], axis=-1)
    bcols = jnp.concatenate([bcols, jnp.zeros(bcols.shape[:2] + (GATE_LANES - 4,), bcols.dtype)], axis=-1)
    t = jnp.arange(tm)
    same = (t[:, None] // SCAN_CHUNK) == (t[None, :] // SCAN_CHUNK)
    tri_f = (same & (t[None, :] <= t[:, None])).astype(BF16)
    tri_b = (same & (t[None, :] >= t[:, None])).astype(BF16)
    return {"w_in": w_in.astype(BF16), "conv_w": conv_w, "conv_b": conv_b[:, None, :],
            "w_q": _dense_blockdiag(w_q), "w_k": _dense_blockdiag(w_k), "w_vt": _dense_blockdiag(w_v, transpose=True),
            "w_if": w_if_t.astype(BF16), "b_if": bcols.reshape(nl, H * GATE_LANES, 1), "tri_f": tri_f, "tri_b": tri_b,
            "head_norm": head_norm[:, None, :], "skip": skip[:, None, :], "w_out": w_out.astype(BF16)}


def _row_tile(t, want):
    tm = min(t, want)
    assert t % tm == 0, (t, tm)
    return tm


def kernel(x, c, ctx, c_ctx, ada_w, ada_b, norm_pre, norm_post, mla_w_in, mla_q_norm, mla_kv_norm, mla_w_uq,
           mla_w_ukv, mla_w_out, ml_w_in, ml_conv_w, ml_conv_b, ml_w_q, ml_w_k, ml_w_v, ml_w_if, ml_b_if,
           ml_head_norm, ml_skip, ml_w_out):
    B, T, D = x.shape
    Tc = ctx.shape[1]
    assert D == D_MODEL and T % GRID_W == 0 and T % SCAN_CHUNK == 0 and Tc % SCAN_CHUNK == 0
    tm_l = _row_tile(T, 1024)
    tm_c = _row_tile(Tc, 256)
    tf_l = _row_tile(T, 256)
    tf_c = _row_tile(Tc, 256)
    tq_l = _row_tile(T, 1024)
    tq_c = _row_tile(Tc, 256)

    n_rows = -(-(B + 1) // 8) * 8
    cc = jnp.concatenate([c, c_ctx[None, :], jnp.zeros((n_rows - B - 1, D), F32)], axis=0)
    mods = _modulation(cc, ada_w, ada_b).reshape(DEPTH, n_rows, 3, D)

    tabq_l, tabk_l = _rope_tables(T, True)
    tabq_c, tabk_c = _rope_tables(Tc, False)
    npre = norm_pre[:, None, :]
    npost = norm_post[:, None, :]
    wa = _prep_mla_weights(mla_w_in, mla_q_norm, mla_kv_norm, mla_w_uq, mla_w_ukv, mla_w_out)
    wb = _prep_ml_weights(ml_w_in, ml_conv_w, ml_conv_b, ml_w_q, ml_w_k, ml_w_v, ml_w_if, ml_b_if, ml_head_norm,
                          ml_skip, ml_w_out, tf_l)
    assert tf_l == tf_c

    for i in range(DEPTH):
        need_ctx_out = i < DEPTH - 1
        j = i // 2
        if i % 2 == 0:
            ql, kl, vtl, sgl = _mla_project(x, mods, None, npre, wa, i, j, tabq_l, tabk_l, tm_l)
            qc, kc, vtc, sgc = _mla_project(ctx, mods, B, npre, wa, i, j, tabq_c, tabk_c, tm_c)
            al = _attention(ql, [kc, kl], [vtc, vtl], sgl, tq_l, 1)
            ac = _attention(qc, [kc], [vtc], sgc, tq_c, MLA_HEADS // 2) if need_ctx_out else None
            w_out = wa["w_out"]
        else:
            streams = []
            for xs, mod_row, tf in ((x, None, tf_l), (ctx, B, tf_c)):
                q, k, vt, xc, sz, og, gc, gr = _ml_front(xs, mods, mod_row, npre, wb, i, j, tf)
                streams.append({"q": q, "k": k, "vt": vt, "gc": gc, "gr": gr, "og": og, "xc": xc, "sz": sz})
            al, ac = _ml_scan(streams[0], streams[1], wb["head_norm"], wb["skip"], j, need_ctx_out)
            w_out = wb["w_out"]
        x = _out_project(al, w_out, j, x, mods, None, npost, i, tm_l)
        if need_ctx_out:
            ctx = _out_project(ac, w_out, j, ctx, mods, B, npost, i, tm_c)
    return x
```

```python
import functools

import jax
import jax.numpy as jnp
from jax import lax
from jax.experimental import pallas as pl
from jax.experimental.pallas import tpu as pltpu

F32 = jnp.float32
BF16 = jnp.bfloat16

D_MODEL = 1024
DEPTH = 4
GRID_W = 64
EPS = 1e-6
ROPE_THETA = 10000.0
MLA_HEADS = 16
QK_NOPE = 64
QK_ROPE = 32
V_HEAD = 64
Q_LORA = 256
KV_LORA = 256
QK_HEAD = QK_NOPE + QK_ROPE
MLA_WIDTH = MLA_HEADS * V_HEAD
ATTN_SCALE = QK_HEAD ** -0.5
LOG2_E = 1.4426950408889634
ML_INNER = 2 * D_MODEL
ML_HEADS = 8
ML_HEAD_DIM = ML_INNER // ML_HEADS
QKV_BLOCK = 4
CONV_K = 5

LANES = 128
MXU_DIM = 256
VMEM_LIMIT = 56 * 1024 * 1024

HEAD_PAD = LANES
MLA_QW = MLA_HEADS * HEAD_PAD
SCAN_CHUNK = MXU_DIM
HALO = 16
GATE_LANES = 8
SCAN_HEADS = 2

NT_DIMS = (((1,), (1,)), ((), ()))


def _cparams(n_axes):
    return pltpu.CompilerParams(dimension_semantics=("parallel",) * n_axes,
                                vmem_limit_bytes=VMEM_LIMIT)


def _resident_spec(shape):
    nd = len(shape)
    return pl.BlockSpec(shape, lambda *_: (0,) * nd, pipeline_mode=pl.Buffered(1))


def _layer_spec(shape, layer, resident=False):
    nd = len(shape)
    kw = {"pipeline_mode": pl.Buffered(1)} if resident else {}
    return pl.BlockSpec((None,) + tuple(shape), lambda *_: (layer,) + (0,) * nd, **kw)


def _mod_spec(layer, row=None):
    if row is None:
        return pl.BlockSpec((None, 1, 3, D_MODEL), lambda b, i: (layer, b, 0, 0))
    return pl.BlockSpec((None, 1, 3, D_MODEL), lambda b, i: (layer, row, 0, 0))


def _rms(x, g):
    r = lax.rsqrt(jnp.mean(x * x, axis=-1, keepdims=True) + EPS)
    return (x * r) * g


def _sigmoid(x):
    return 0.5 * jnp.tanh(0.5 * x) + 0.5


def _silu(x):
    return x * _sigmoid(x)


def _split3(x):
    x1 = x.astype(BF16)
    r1 = x - x1.astype(F32)
    x2 = r1.astype(BF16)
    x3 = (r1 - x2.astype(F32)).astype(BF16)
    return x1, x2, x3


def _dot(a, b):
    return jnp.dot(a, b, preferred_element_type=F32)


def _mod_kernel(c_ref, w_ref, b_ref, o_ref):
    a1, a2, a3 = _split3(_silu(c_ref[...]))
    w = w_ref[0]
    w1 = w.astype(BF16)
    w2 = (w - w1.astype(F32)).astype(BF16)
    acc = _dot(a1, w1) + (_dot(a1, w2) + _dot(a2, w1)) + (_dot(a2, w2) + _dot(a3, w1))
    o_ref[0] = acc + b_ref[0]


def _modulation(cc, ada_w, ada_b):
    rows = cc.shape[0]
    nblk = ada_w.shape[2] // D_MODEL
    return pl.pallas_call(
        _mod_kernel,
        grid=(DEPTH, nblk),
        in_specs=[pl.BlockSpec((rows, D_MODEL), lambda i, j: (0, 0)),
                  pl.BlockSpec((1, D_MODEL, D_MODEL), lambda i, j: (i, 0, j)),
                  pl.BlockSpec((1, 1, D_MODEL), lambda i, j: (i, 0, j))],
        out_specs=pl.BlockSpec((1, rows, D_MODEL), lambda i, j: (i, 0, j)),
        out_shape=jax.ShapeDtypeStruct((DEPTH, rows, ada_w.shape[2]), F32),
        compiler_params=_cparams(2),
        name="modulation",
    )(cc, ada_w, ada_b.reshape(DEPTH, 1, -1))


def _prenorm(x_ref, mod_ref, npre_ref):
    h = _rms(x_ref[0], npre_ref[...]) * (1.0 + mod_ref[0, 1:2, :]) + mod_ref[0, 0:1, :]
    return h.astype(BF16)


def _mla_proj_kernel(x_ref, mod_ref, npre_ref, win_ref, qn_ref, kvn_ref, wuq_ref, wuk_ref, wuvt_ref,
                     tabq_ref, tabk_ref, q_ref, k_ref, vt_ref, sg_ref):
    h = _prenorm(x_ref, mod_ref, npre_ref)
    u = _dot(h, win_ref[...])
    cq = u[:, :Q_LORA]
    ckv = u[:, Q_LORA:Q_LORA + KV_LORA]
    krp = u[:, Q_LORA + KV_LORA:Q_LORA + KV_LORA + LANES]
    gate = u[:, Q_LORA + KV_LORA + LANES:]
    ckvn = _rms(ckv, kvn_ref[...]).astype(BF16)
    q = _dot(_rms(cq, qn_ref[...]).astype(BF16), wuq_ref[...])
    kn = _dot(ckvn, wuk_ref[...])
    vt_ref[0] = lax.dot_general(wuvt_ref[...], ckvn, NT_DIMS, preferred_element_type=F32).astype(BF16)
    t = krp * tabk_ref[...]
    r = t + pltpu.roll(t, QK_ROPE, 1)
    lane = lax.broadcasted_iota(jnp.int32, r.shape, 1)
    rk = jnp.where(lane >= QK_NOPE, r, 0.0)
    tabq = tabq_ref[...]
    for hh in range(MLA_HEADS):
        sl = slice(HEAD_PAD * hh, HEAD_PAD * (hh + 1))
        q_ref[0, :, sl] = (q[:, sl] * tabq).astype(BF16)
        k_ref[0, :, sl] = (kn[:, sl] + rk).astype(BF16)
    sg_ref[0] = _silu(gate).astype(BF16)


def _mla_project(x, mods, mod_row, npre, w, layer, j, tabq, tabk, tm):
    B, T, D = x.shape
    nin = w["w_in"].shape[-1]
    row = lambda width: pl.BlockSpec((1, tm, width), lambda b, i: (b, i, 0))
    return pl.pallas_call(
        _mla_proj_kernel,
        grid=(B, T // tm),
        in_specs=[row(D),
                  _mod_spec(layer, mod_row),
                  _layer_spec((1, D), layer),
                  _layer_spec((D, nin), j),
                  _layer_spec((1, Q_LORA), j),
                  _layer_spec((1, KV_LORA), j),
                  _layer_spec((Q_LORA, MLA_QW), j),
                  _layer_spec((KV_LORA, MLA_QW), j),
                  _layer_spec((MLA_WIDTH, KV_LORA), j),
                  pl.BlockSpec((tm, LANES), lambda b, i: (i, 0)),
                  pl.BlockSpec((tm, LANES), lambda b, i: (i, 0))],
        out_specs=[row(MLA_QW), row(MLA_QW),
                   pl.BlockSpec((1, MLA_WIDTH, tm), lambda b, i: (b, 0, i)),
                   row(MLA_WIDTH)],
        out_shape=[jax.ShapeDtypeStruct((B, T, MLA_QW), BF16),
                   jax.ShapeDtypeStruct((B, T, MLA_QW), BF16),
                   jax.ShapeDtypeStruct((B, MLA_WIDTH, T), BF16),
                   jax.ShapeDtypeStruct((B, T, MLA_WIDTH), BF16)],
        compiler_params=_cparams(2),
        name="mla_project",
    )(x, mods, npre, w["w_in"], w["q_norm"], w["kv_norm"], w["w_uq"], w["w_uk"], w["w_uvt"], tabq, tabk)


VAUG_ROWS = V_HEAD + 16


def _attn_kernel(*refs, n_seg, tq, n_pairs):
    q_ref = refs[0]
    k_refs = refs[1:1 + n_seg]
    vt_refs = refs[1 + n_seg:1 + 2 * n_seg]
    sg_ref = refs[1 + 2 * n_seg]
    o_ref = refs[2 + 2 * n_seg]
    va_refs = refs[3 + 2 * n_seg:3 + 3 * n_seg]
    ot_ref = refs[3 + 3 * n_seg]
    n_q = q_ref.shape[1] // tq

    for vt_ref, va_ref in zip(vt_refs, va_refs):
        tk = vt_ref.shape[2]
        ones_row = jnp.where(lax.broadcasted_iota(jnp.int32, (VAUG_ROWS - V_HEAD, tk), 0) == 0, 1.0, 0.0).astype(BF16)
        for h in range(2 * n_pairs):
            va_ref[h, :V_HEAD, :] = vt_ref[0, V_HEAD * h:V_HEAD * (h + 1), :]
            va_ref[h, V_HEAD:, :] = ones_row

    units = [(p, j, hh) for p in range(n_pairs) for j in range(n_q) for hh in range(2)]

    def scores(u):
        p, j, hh = u
        rows = slice(j * tq, (j + 1) * tq)
        sl = slice(HEAD_PAD * (2 * p + hh), HEAD_PAD * (2 * p + hh + 1))
        qb = q_ref[0, rows, sl]
        ss, m = [], None
        for k in k_refs:
            s = lax.dot_general(k[0, :, sl], qb, NT_DIMS, preferred_element_type=F32)
            ss.append(s)
            mm = jnp.max(s, axis=0, keepdims=True)
            m = mm if m is None else jnp.maximum(m, mm)
        return ss, m

    def values(u, sm):
        p, j, hh = u
        rows = slice(j * tq, (j + 1) * tq)
        ss, m = sm
        oa = None
        for s, va in zip(ss, va_refs):
            t = _dot(va[2 * p + hh], jnp.exp2(s - m).astype(BF16))
            oa = t if oa is None else oa + t
        ot_ref[V_HEAD * hh:V_HEAD * (hh + 1), :] = oa[:V_HEAD] * (1.0 / oa[V_HEAD:V_HEAD + 1])
        if hh == 1:
            cols = slice(2 * V_HEAD * p, 2 * V_HEAD * (p + 1))
            o_ref[0, rows, cols] = (ot_ref[...].T * sg_ref[0, rows, cols].astype(F32)).astype(BF16)

    prev = scores(units[0])
    for u_prev, u in zip(units[:-1], units[1:]):
        cur = scores(u)
        values(u_prev, prev)
        prev = cur
    values(units[-1], prev)


def _attention(q, ks, vts, sg, tq, n_pairs):
    B, Tq, _ = q.shape
    n_seg = len(ks)
    blk = lambda t, width: pl.BlockSpec((1, t, width * n_pairs), lambda b, p: (b, 0, p))
    return pl.pallas_call(
        functools.partial(_attn_kernel, n_seg=n_seg, tq=tq, n_pairs=n_pairs),
        grid=(B, MLA_HEADS // (2 * n_pairs)),
        in_specs=([blk(Tq, 2 * HEAD_PAD)]
                  + [blk(k.shape[1], 2 * HEAD_PAD) for k in ks]
                  + [pl.BlockSpec((1, 2 * V_HEAD * n_pairs, vt.shape[2]), lambda b, p: (b, p, 0)) for vt in vts]
                  + [blk(Tq, 2 * V_HEAD)]),
        out_specs=blk(Tq, 2 * V_HEAD),
        out_shape=jax.ShapeDtypeStruct((B, Tq, MLA_WIDTH), BF16),
        scratch_shapes=([pltpu.VMEM((2 * n_pairs, VAUG_ROWS, vt.shape[2]), BF16) for vt in vts]
                        + [pltpu.VMEM((2 * V_HEAD, tq), F32)]),
        compiler_params=_cparams(2),
        name="mla_attention",
    )(q, *ks, *vts, sg)


def _out_kernel(a_ref, w_ref, x_ref, mod_ref, npost_ref, o_ref):
    y = _dot(a_ref[0], w_ref[...])
    o_ref[0] = x_ref[0] + mod_ref[0, 2:3, :] * _rms(y, npost_ref[...])


def _out_project(a, w_out, j, x, mods, mod_row, npost, layer, tm):
    B, T, K = a.shape
    D = x.shape[-1]
    return pl.pallas_call(
        _out_kernel,
        grid=(B, T // tm),
        in_specs=[pl.BlockSpec((1, tm, K), lambda b, i: (b, i, 0)),
                  _layer_spec((K, D), j),
                  pl.BlockSpec((1, tm, D), lambda b, i: (b, i, 0)),
                  _mod_spec(layer, mod_row),
                  _layer_spec((1, D), layer)],
        out_specs=pl.BlockSpec((1, tm, D), lambda b, i: (b, i, 0)),
        out_shape=jax.ShapeDtypeStruct(x.shape, F32),
        compiler_params=_cparams(2),
        name="out_project",
    )(a, w_out, x, mods, npost)


def _ml_front_kernel(x_ref, xp_ref, xn_ref, mod_ref, npre_ref, win_ref, cw_ref, cb_ref, wq_ref, wk_ref,
                     wvt_ref, wif_ref, bif_ref, trif_ref, trib_ref,
                     q_ref, k_ref, vt_ref, xc_ref, sz_ref, og_ref, gc_ref, gr_ref, xe_ref, *, tm):
    i = pl.program_id(1)
    last = pl.num_programs(1) - 1
    pad = CONV_K // 2
    half = HALO // 2

    def prenorm(xv):
        return _rms(xv, npre_ref[...]) * (1.0 + mod_ref[0, 1:2, :]) + mod_ref[0, 0:1, :]

    hx = jnp.concatenate([prenorm(xp_ref[0]), prenorm(x_ref[0]), prenorm(xn_ref[0])], axis=0).astype(BF16)
    xme = _dot(hx, win_ref[:, :ML_INNER])
    xe_ref[0:half, :] = jnp.where(i > 0, xme[0:half], 0.0)
    xe_ref[half:half + tm, :] = xme[half:half + tm]
    xe_ref[half + tm:, :] = jnp.where(i < last, xme[half + tm:], 0.0)
    h = hx[half:half + tm]
    xm = xme[half:half + tm].astype(BF16)

    xe = xe_ref[...]
    acc = cb_ref[...]
    for j in range(CONV_K):
        sh = (pad - j) % (tm + HALO)
        xs = xe if sh == 0 else pltpu.roll(xe, sh, 0)
        acc = acc + cw_ref[j:j + 1, :] * xs[half:half + tm, :]
    xc = _silu(acc)
    xcb = xc.astype(BF16)
    xc_ref[0] = xcb

    n_tiles = ML_INNER // MXU_DIM
    n_gate = ML_HEADS * GATE_LANES
    gt = jnp.zeros((n_gate, tm), F32) + bif_ref[...]
    for j in range(n_tiles):
        sl = slice(MXU_DIM * j, MXU_DIM * (j + 1))
        qj = _dot(xcb[:, sl], wq_ref[j]).astype(BF16)
        kj = _dot(xcb[:, sl], wk_ref[j])
        vtj = lax.dot_general(wvt_ref[j], xm[:, sl], NT_DIMS, preferred_element_type=F32).astype(BF16)
        q_ref[0, :, sl] = qj
        k_ref[0, :, sl] = (kj * (ML_HEAD_DIM ** -0.5)).astype(BF16)
        vt_ref[0, sl, :] = vtj
        gt = gt + lax.dot_general(wif_ref[:, sl], qj, NT_DIMS, preferred_element_type=F32)
        gt = gt + lax.dot_general(wif_ref[:, ML_INNER + MXU_DIM * j:ML_INNER + MXU_DIM * (j + 1)], kj.astype(BF16),
                                  NT_DIMS, preferred_element_type=F32)
        gt = gt + _dot(wif_ref[:, 2 * ML_INNER + MXU_DIM * j:2 * ML_INNER + MXU_DIM * (j + 1)], vtj)

    sz_ref[0] = _silu(_dot(h, win_ref[:, ML_INNER:2 * ML_INNER])).astype(BF16)
    og_ref[0] = _sigmoid(_dot(h, win_ref[:, 2 * ML_INNER:])).astype(BF16)

    sub = lax.broadcasted_iota(jnp.int32, gt.shape, 0) % GATE_LANES
    is_f_fwd = sub == 1
    is_f_bwd = sub == 3
    logsig = jnp.minimum(gt, 0.0) - jnp.log(1.0 + jnp.exp(-jnp.abs(gt)))
    gs = jnp.where(is_f_fwd | is_f_bwd, logsig, gt)
    g1, g2, g3 = _split3(gs)
    cum_f = _dot(g1, trib_ref[...]) + _dot(g2, trib_ref[...]) + _dot(g3, trib_ref[...])
    cum_b = _dot(g1, trif_ref[...]) + _dot(g2, trif_ref[...]) + _dot(g3, trif_ref[...])
    out_t = jnp.where(is_f_fwd, cum_f, jnp.where(is_f_bwd, cum_b, gs))
    gr_ref[0] = out_t
    out = jnp.concatenate([out_t, jnp.zeros((LANES - n_gate, tm), F32)], axis=0).T
    diff = out - pltpu.roll(out, LANES - 1, 1)
    for hd in range(ML_HEADS):
        gc_ref[0, hd] = diff[:, GATE_LANES * hd:GATE_LANES * (hd + 1)]


def _ml_front(x, mods, mod_row, npre, w, layer, j, tm):
    B, T, D = x.shape
    C = ML_INNER
    half = HALO // 2
    nh = tm // half
    n_halo = T // half
    row = lambda width: pl.BlockSpec((1, tm, width), lambda b, i: (b, i, 0))
    n_tiles = C // MXU_DIM
    n_gate = ML_HEADS * GATE_LANES
    tile_w = _layer_spec((n_tiles, MXU_DIM, MXU_DIM), j, resident=True)
    return pl.pallas_call(
        functools.partial(_ml_front_kernel, tm=tm),
        grid=(B, T // tm),
        in_specs=[row(D),
                  pl.BlockSpec((1, half, D), lambda b, i: (b, jnp.maximum(i * nh - 1, 0), 0)),
                  pl.BlockSpec((1, half, D), lambda b, i: (b, jnp.minimum((i + 1) * nh, n_halo - 1), 0)),
                  _mod_spec(layer, mod_row),
                  _layer_spec((1, D), layer, resident=True),
                  _layer_spec((D, 3 * C), j, resident=True),
                  _layer_spec((CONV_K, C), j, resident=True),
                  _layer_spec((1, C), j, resident=True),
                  tile_w, tile_w, tile_w,
                  _layer_spec((n_gate, 3 * C), j, resident=True),
                  _layer_spec((n_gate, 1), j, resident=True),
                  _resident_spec((tm, tm)),
                  _resident_spec((tm, tm))],
        out_specs=[row(C), row(C),
                   pl.BlockSpec((1, C, tm), lambda b, i: (b, 0, i)),
                   row(C), row(C), row(C),
                   pl.BlockSpec((1, ML_HEADS, tm, GATE_LANES), lambda b, i: (b, 0, i, 0)),
                   pl.BlockSpec((1, n_gate, tm), lambda b, i: (b, 0, i))],
        out_shape=[jax.ShapeDtypeStruct((B, T, C), BF16),
                   jax.ShapeDtypeStruct((B, T, C), BF16),
                   jax.ShapeDtypeStruct((B, C, T), BF16),
                   jax.ShapeDtypeStruct((B, T, C), BF16),
                   jax.ShapeDtypeStruct((B, T, C), BF16),
                   jax.ShapeDtypeStruct((B, T, C), BF16),
                   jax.ShapeDtypeStruct((B, ML_HEADS, T, GATE_LANES), F32),
                   jax.ShapeDtypeStruct((B, n_gate, T), F32)],
        scratch_shapes=[pltpu.VMEM((tm + HALO, C), F32)],
        compiler_params=_cparams(2),
        name="mlstm_front",
    )(x, x, x, mods, npre, w["w_in"], w["conv_w"], w["conv_b"], w["w_q"], w["w_k"], w["w_vt"],
      w["w_if"], w["b_if"], w["tri_f"], w["tri_b"])


STATE_ROWS = ML_HEAD_DIM + 16


def _scan_chunk(q_ref, k_ref, vt_ref, gc_ref, gr_ref, ht_ref, cn_ref, m, c, hh, d, with_out, first_visit):
    L = SCAN_CHUNK
    dh = ML_HEAD_DIM
    rows = slice(c * L, (c + 1) * L)
    hl = slice(dh * hh, dh * (hh + 1))
    kk = k_ref[0, rows, hl]
    vt = vt_ref[0, hl, rows]
    g0 = GATE_LANES * hh + 2 * d
    ig_r = gr_ref[0, g0:g0 + 1, rows]
    b_r = gr_ref[0, g0 + 1:g0 + 2, rows]
    b_end = b_r[:, L - 1:L] if d == 0 else b_r[:, 0:1]
    w_r = b_end - b_r + ig_r
    m_new = jnp.maximum(b_end + m, jnp.max(w_r, axis=-1, keepdims=True))
    a_r = jnp.exp(w_r - m_new)
    decay = jnp.exp(b_end + m - m_new)
    cn = cn_ref[hh, d]
    if with_out:
        qq = q_ref[0, rows, hl]
        u_c = gc_ref[0, hh, rows, 2 * d:2 * d + 1]
        si = lax.broadcasted_iota(jnp.int32, (L, L), 0)
        ti = lax.broadcasted_iota(jnp.int32, (L, L), 1)
        mask = (si <= ti) if d == 0 else (si >= ti)
        log_d = jnp.where(mask, u_c + b_r, -jnp.inf)
        m_t = jnp.maximum(b_r + m, jnp.max(log_d, axis=0, keepdims=True))
        inter = jnp.exp(b_r + m - m_t)
        s = lax.dot_general(kk, qq, NT_DIMS, preferred_element_type=F32) * jnp.exp(log_d - m_t)
        cq = lax.dot_general(cn.astype(BF16), qq, NT_DIMS, preferred_element_type=F32)
        num = _dot(vt, s.astype(BF16)) + inter * cq[:dh]
        den = jnp.sum(s, axis=0, keepdims=True) + inter * cq[dh:dh + 1]
        hh_t = num * (1.0 / jnp.maximum(jnp.abs(den), jnp.exp(-m_t)))
        if first_visit:
            ht_ref[hl, rows] = hh_t
        else:
            ht_ref[hl, rows] += hh_t
    sub = lax.broadcasted_iota(jnp.int32, (STATE_ROWS - dh, L), 0)
    va = jnp.concatenate([vt * a_r.astype(BF16),
                          jnp.where(sub == 0, a_r, 0.0).astype(BF16)], axis=0)
    cn_ref[hh, d] = decay * cn + _dot(va, kk)
    return m_new


def _scan_finish(ht_ref, og_ref, xc_ref, sz_ref, hn_ref, skip_ref, y_ref):
    L = SCAN_CHUNK
    dh = ML_HEAD_DIM

    def body(i, carry):
        rows = pl.ds(pl.multiple_of(i * L, L), L)
        for hh in range(SCAN_HEADS):
            hl = slice(dh * hh, dh * (hh + 1))
            h = og_ref[0, rows, hl].astype(F32) * ht_ref[hl, rows].T
            hn = h * lax.rsqrt(jnp.mean(h * h, axis=-1, keepdims=True) + EPS)
            y = ((hn * hn_ref[:, hl] + skip_ref[:, hl] * xc_ref[0, rows, hl].astype(F32))
                 * sz_ref[0, rows, hl].astype(F32))
            y_ref[0, rows, hl] = y.astype(BF16)
        return carry

    lax.fori_loop(0, ht_ref.shape[1] // L, body, 0, unroll=True)


def _ml_scan_kernel(*refs, ctx_out):
    (ql, kl, vtl, gcl, grl, ogl, xcl, szl, qc, kc, vtc, gcc, grc) = refs[:13]
    pos = 13
    if ctx_out:
        ogc, xcc, szc = refs[pos:pos + 3]
        pos += 3
    hn_ref, skip_ref = refs[pos:pos + 2]
    pos += 2
    yl_ref = refs[pos]
    pos += 1
    if ctx_out:
        yc_ref = refs[pos]
        pos += 1
    hl_ref, hc_ref, cn_ref = refs[pos:pos + 3]
    L = SCAN_CHUNK
    n_l = ql.shape[1] // L
    n_c = kc.shape[1] // L
    chains = [(hh, d) for hh in range(SCAN_HEADS) for d in range(2)]

    cn_ref[...] = jnp.zeros_like(cn_ref)
    ms = [jnp.zeros((1, 1), F32) for _ in chains]

    def sweep(refs, h_ref, n, with_out, ms):
        for i in range(n):
            new_ms = []
            for (hh, d), m in zip(chains, ms):
                c = i if d == 0 else n - 1 - i
                first = (2 * i + 1 < n) or (2 * i + 1 == n and d == 0)
                new_ms.append(_scan_chunk(*refs, h_ref, cn_ref, m, c, hh, d, with_out, first))
            ms = new_ms
        return ms

    ms = sweep((qc, kc, vtc, gcc, grc), hc_ref, n_c, ctx_out, ms)
    sweep((ql, kl, vtl, gcl, grl), hl_ref, n_l, True, ms)

    _scan_finish(hl_ref, ogl, xcl, szl, hn_ref, skip_ref, yl_ref)
    if ctx_out:
        _scan_finish(hc_ref, ogc, xcc, szc, hn_ref, skip_ref, yc_ref)


def _ml_scan(lat, ctx, head_norm, skip, j, ctx_out):
    B, T, C = lat["q"].shape
    Tc = ctx["q"].shape[1]
    wd = ML_HEAD_DIM * SCAN_HEADS

    def stream_specs(t, names):
        specs = []
        for nm in names:
            if nm == "gc":
                specs.append(pl.BlockSpec((1, SCAN_HEADS, t, GATE_LANES), lambda b, h: (b, h, 0, 0)))
            elif nm == "gr":
                specs.append(pl.BlockSpec((1, SCAN_HEADS * GATE_LANES, t), lambda b, h: (b, h, 0)))
            elif nm == "vt":
                specs.append(pl.BlockSpec((1, wd, t), lambda b, h: (b, h, 0)))
            else:
                specs.append(pl.BlockSpec((1, t, wd), lambda b, h: (b, 0, h)))
        return specs

    lat_names = ["q", "k", "vt", "gc", "gr", "og", "xc", "sz"]
    ctx_names = ["q", "k", "vt", "gc", "gr"] + (["og", "xc", "sz"] if ctx_out else [])
    head_vec = pl.BlockSpec((None, 1, wd), lambda b, h: (j, 0, h))
    out_specs = [pl.BlockSpec((1, T, wd), lambda b, h: (b, 0, h))]
    out_shape = [jax.ShapeDtypeStruct((B, T, C), BF16)]
    if ctx_out:
        out_specs.append(pl.BlockSpec((1, Tc, wd), lambda b, h: (b, 0, h)))
        out_shape.append(jax.ShapeDtypeStruct((B, Tc, C), BF16))
    outs = pl.pallas_call(
        functools.partial(_ml_scan_kernel, ctx_out=ctx_out),
        grid=(B, ML_HEADS // SCAN_HEADS),
        in_specs=stream_specs(T, lat_names) + stream_specs(Tc, ctx_names) + [head_vec, head_vec],
        out_specs=out_specs,
        out_shape=out_shape,
        scratch_shapes=[pltpu.VMEM((wd, T), F32), pltpu.VMEM((wd, Tc), F32),
                        pltpu.VMEM((SCAN_HEADS, 2, STATE_ROWS, ML_HEAD_DIM), F32)],
        compiler_params=_cparams(2),
        name="mlstm_scan",
    )(*[lat[nm] for nm in lat_names], *[ctx[nm] for nm in ctx_names], head_norm, skip)
    return (outs[0], outs[1]) if ctx_out else (outs[0], None)


def _swap_pairs(a):
    g = a.reshape(a.shape[:-1] + (2, 2, QK_ROPE // 4))
    return jnp.flip(g, axis=-2).reshape(a.shape)


def _prep_mla_weights(w_in, q_norm, kv_norm, w_uq, w_ukv, w_out):
    A = w_in.shape[0]
    kr0 = Q_LORA + KV_LORA
    kr = w_in[..., kr0:kr0 + QK_ROPE]
    krs = _swap_pairs(kr)
    w_in_p = jnp.concatenate([w_in[..., :kr0], kr, krs, kr, krs, w_in[..., kr0 + QK_ROPE:]], axis=-1)
    uq = w_uq.reshape(A, Q_LORA, MLA_HEADS, QK_HEAD)
    rope = uq[..., QK_NOPE:]
    uq_p = jnp.concatenate([uq[..., :QK_NOPE], rope, _swap_pairs(rope)], axis=-1).reshape(A, Q_LORA, MLA_QW)
    ukv = w_ukv.reshape(A, KV_LORA, MLA_HEADS, QK_NOPE + V_HEAD)
    uk_p = jnp.concatenate([ukv[..., :QK_NOPE], jnp.zeros((A, KV_LORA, MLA_HEADS, HEAD_PAD - QK_NOPE), w_ukv.dtype)],
                           axis=-1).reshape(A, KV_LORA, MLA_QW)
    uvt = jnp.swapaxes(ukv[..., QK_NOPE:].reshape(A, KV_LORA, MLA_WIDTH), 1, 2)
    return {"w_in": w_in_p.astype(BF16), "q_norm": q_norm[:, None, :], "kv_norm": kv_norm[:, None, :],
            "w_uq": uq_p.astype(BF16), "w_uk": uk_p.astype(BF16), "w_uvt": uvt.astype(BF16),
            "w_out": w_out.astype(BF16)}


def _rope_tables(n_tokens, rotate):
    if rotate:
        rows = n_tokens // GRID_W
        row = jnp.repeat(jnp.arange(rows, dtype=jnp.int32), GRID_W).astype(F32)
        col = jnp.tile(jnp.arange(GRID_W, dtype=jnp.int32), rows).astype(F32)
        qd = QK_ROPE // 4
        inv = ROPE_THETA ** (-jnp.arange(qd, dtype=F32) / qd)
        ang = jnp.stack([row[:, None] * inv, col[:, None] * inv], axis=1)
        cos, sin = jnp.cos(ang), jnp.sin(ang)
    else:
        cos = jnp.ones((n_tokens, 2, QK_ROPE // 4), F32)
        sin = jnp.zeros((n_tokens, 2, QK_ROPE // 4), F32)
    cf = jnp.stack([cos, cos], axis=2).reshape(n_tokens, QK_ROPE)
    sf = jnp.stack([-sin, sin], axis=2).reshape(n_tokens, QK_ROPE)
    tabq = (ATTN_SCALE * LOG2_E) * jnp.concatenate([jnp.ones((n_tokens, QK_NOPE), F32), cf, sf], axis=1)
    tabk = jnp.concatenate([cf, sf, cf, sf], axis=1)
    return tabq, tabk


def _dense_blockdiag(w, transpose=False):
    per = MXU_DIM // QKV_BLOCK
    if transpose:
        w = jnp.swapaxes(w, -1, -2)
    rows = w.reshape(w.shape[0], -1, MXU_DIM, 1, QKV_BLOCK)
    full = jnp.broadcast_to(rows, rows.shape[:3] + (per, QKV_BLOCK)).reshape(w.shape[0], -1, MXU_DIM, MXU_DIM)
    r = lax.broadcasted_iota(jnp.int32, (MXU_DIM, MXU_DIM), 0) // QKV_BLOCK
    c = lax.broadcasted_iota(jnp.int32, (MXU_DIM, MXU_DIM), 1) // QKV_BLOCK
    return jnp.where(r == c, full, 0.0).astype(BF16)


def _prep_ml_weights(w_in, conv_w, conv_b, w_q, w_k, w_v, w_if, b_if, head_norm, skip, w_out, tm):
    H = ML_HEADS
    nl = w_in.shape[0]
    cols = jnp.stack([w_if[:, 0, :, :H], w_if[:, 0, :, H:], w_if[:, 1, :, :H], w_if[:, 1, :, H:]], axis=-1)
    cols = jnp.concatenate([cols, jnp.zeros(cols.shape[:3] + (GATE_LANES - 4,), cols.dtype)], axis=-1)
    w_if_t = jnp.swapaxes(cols.reshape(nl, cols.shape[1], H * GATE_LANES), 1, 2)
    bcols = jnp.stack([b_if[:, 0, :H], b_if[:, 0, H:], b_if[:, 1, :H], b_if[:, 1, H:]], axis=-1)
    bcols = jnp.concatenate([bcols, jnp.zeros(bcols.shape[:2] + (GATE_LANES - 4,), bcols.dtype)], axis=-1)
    t = jnp.arange(tm)
    same = (t[:, None] // SCAN_CHUNK) == (t[None, :] // SCAN_CHUNK)
    tri_f = (same & (t[None, :] <= t[:, None])).astype(BF16)
    tri_b = (same & (t[None, :] >= t[:, None])).astype(BF16)
    return {"w_in": w_in.astype(BF16), "conv_w": conv_w, "conv_b": conv_b[:, None, :],
            "w_q": _dense_blockdiag(w_q), "w_k": _dense_blockdiag(w_k), "w_vt": _dense_blockdiag(w_v, transpose=True),
            "w_if": w_if_t.astype(BF16), "b_if": bcols.reshape(nl, H * GATE_LANES, 1), "tri_f": tri_f, "tri_b": tri_b,
            "head_norm": head_norm[:, None, :], "skip": skip[:, None, :], "w_out": w_out.astype(BF16)}


def _row_tile(t, want):
    tm = min(t, want)
    assert t % tm == 0, (t, tm)
    return tm


def kernel(x, c, ctx, c_ctx, ada_w, ada_b, norm_pre, norm_post, mla_w_in, mla_q_norm, mla_kv_norm, mla_w_uq,
           mla_w_ukv, mla_w_out, ml_w_in, ml_conv_w, ml_conv_b, ml_w_q, ml_w_k, ml_w_v, ml_w_if, ml_b_if,
           ml_head_norm, ml_skip, ml_w_out):
    B, T, D = x.shape
    Tc = ctx.shape[1]
    assert D == D_MODEL and T % GRID_W == 0 and T % SCAN_CHUNK == 0 and Tc % SCAN_CHUNK == 0
    tm_l = _row_tile(T, 1024)
    tm_c = _row_tile(Tc, 256)
    tf_l = _row_tile(T, 256)
    tf_c = _row_tile(Tc, 256)
    tq_l = _row_tile(T, 1024)
    tq_c = _row_tile(Tc, 256)

    n_rows = -(-(B + 1) // 8) * 8
    cc = jnp.concatenate([c, c_ctx[None, :], jnp.zeros((n_rows - B - 1, D), F32)], axis=0)
    mods = _modulation(cc, ada_w, ada_b).reshape(DEPTH, n_rows, 3, D)

    tabq_l, tabk_l = _rope_tables(T, True)
    tabq_c, tabk_c = _rope_tables(Tc, False)
    npre = norm_pre[:, None, :]
    npost = norm_post[:, None, :]
    wa = _prep_mla_weights(mla_w_in, mla_q_norm, mla_kv_norm, mla_w_uq, mla_w_ukv, mla_w_out)
    wb = _prep_ml_weights(ml_w_in, ml_conv_w, ml_conv_b, ml_w_q, ml_w_k, ml_w_v, ml_w_if, ml_b_if, ml_head_norm,
                          ml_skip, ml_w_out, tf_l)
    assert tf_l == tf_c

    for i in range(DEPTH):
        need_ctx_out = i < DEPTH - 1
        j = i // 2
        if i % 2 == 0:
            ql, kl, vtl, sgl = _mla_project(x, mods, None, npre, wa, i, j, tabq_l, tabk_l, tm_l)
            qc, kc, vtc, sgc = _mla_project(ctx, mods, B, npre, wa, i, j, tabq_c, tabk_c, tm_c)
            al = _attention(ql, [kc, kl], [vtc, vtl], sgl, tq_l, 4)
            ac = _attention(qc, [kc], [vtc], sgc, tq_c, MLA_HEADS // 2) if need_ctx_out else None
            w_out = wa["w_out"]
        else:
            streams = []
            for xs, mod_row, tf in ((x, None, tf_l), (ctx, B, tf_c)):
                q, k, vt, xc, sz, og, gc, gr = _ml_front(xs, mods, mod_row, npre, wb, i, j, tf)
                streams.append({"q": q, "k": k, "vt": vt, "gc": gc, "gr": gr, "og": og, "xc": xc, "sz": sz})
            al, ac = _ml_scan(streams[0], streams[1], wb["head_norm"], wb["skip"], j, need_ctx_out)
            w_out = wb["w_out"]
        x = _out_project(al, w_out, j, x, mods, None, npost, i, tm_l)
        if need_ctx_out:
            ctx = _out_project(ac, w_out, j, ctx, mods, B, npost, i, tm_c)
    return x
```
